```python
import math
import jax
import jax.numpy as jnp
from jax import lax
import numpy as np

D_MODEL = 4096
BATCH = 4
SEQ = 4096
DEPTH = 2

D_MIX = D_MODEL
MIX_W = D_MIX // 4
RMS_EPS = 1e-6
D_FF = ((8 * D_MODEL // 3 + 255) // 256) * 256
FFN_CONV = 3

SSD_HEAD_DIM = 64
SSD_HEADS = MIX_W // SSD_HEAD_DIM
SSD_STATE = 128
SSD_GROUPS = 2
SSD_CONV = 4
SSD_CHUNK = 64
SSD_XBC = MIX_W + 2 * SSD_GROUPS * SSD_STATE

S5_GROUP_CH = 16
S5_GROUPS = MIX_W // S5_GROUP_CH
S5_STATE = 64
S5_DT_MIN = 1e-3
S5_DT_MAX = 1e-1

GLA_HEADS = 4
GLA_DK = MIX_W // (2 * GLA_HEADS)
GLA_DV = MIX_W // GLA_HEADS
GLA_GATE_RANK = 16
GLA_GATE_NORM = 16.0
GLA_CHUNK = 16

GDN_HEADS = 8
GDN_DK = MIX_W // GDN_HEADS
GDN_DV = MIX_W // GDN_HEADS
GDN_CONV = 4
GDN_CHUNK = 64

SPLIT_SIZES = (
    MIX_W, SSD_XBC, SSD_HEADS,
    MIX_W,
    GLA_HEADS * GLA_DK, GLA_HEADS * GLA_DK, MIX_W, MIX_W, GLA_GATE_RANK,
    GDN_HEADS * GDN_DK, GDN_HEADS * GDN_DK, GDN_HEADS * GDN_DV, MIX_W, GDN_HEADS, GDN_HEADS,
)
D_IN_PROJ = sum(SPLIT_SIZES)

kernel_name = 'hybrid_parallel_ssd_s5_gla_gdn_convffn'


def _rms(x, w):
    xf = x.astype(jnp.float32)
    return xf * lax.rsqrt(jnp.mean(xf * xf, axis=-1, keepdims=True) + RMS_EPS) * w.astype(jnp.float32)


def rmsnorm(x, w):
    return _rms(x, w).astype(x.dtype)


def _l2norm(x):
    return x * lax.rsqrt(jnp.sum(x * x, axis=-1, keepdims=True) + 1e-6)


def _tril(n, strict=False):
    return jnp.tril(jnp.ones((n, n), dtype=bool), k=-1 if strict else 0)


def causal_dwconv(x, w):
    k = w.shape[0]
    return lax.conv_general_dilated(
        x, w[:, None, :].astype(x.dtype), window_strides=(1,), padding=((k - 1, 0),),
        dimension_numbers=('NWC', 'WIO', 'NWC'), feature_group_count=x.shape[-1])


def _split_columns(proj):
    offsets = []
    total = 0
    for size in SPLIT_SIZES[:-1]:
        total += size
        offsets.append(total)
    return jnp.split(proj, offsets, axis=-1)


def _inter_chunk_states(chunk_states, chunk_decay):
    def step(state, inp):
        s_c, d_c = inp
        return state * d_c + s_c, state
    init = jnp.zeros_like(chunk_states[:, 0])
    _, prev = lax.scan(step, init, (jnp.moveaxis(chunk_states, 1, 0), jnp.moveaxis(chunk_decay, 1, 0)))
    return jnp.moveaxis(prev, 0, 1)


def ssd_mixer(z, xbc, dt_raw, conv_w, conv_b, dt_bias, a_log, d_skip, norm_w):
    bsz, seq, _ = z.shape
    nc, q = seq // SSD_CHUNK, SSD_CHUNK
    g, r = SSD_GROUPS, SSD_HEADS // SSD_GROUPS
    xbc = jax.nn.silu((causal_dwconv(xbc, conv_w) + conv_b).astype(jnp.float32))
    xs = xbc[..., :MIX_W].reshape(bsz, nc, q, g, r, SSD_HEAD_DIM)
    bm = xbc[..., MIX_W:MIX_W + g * SSD_STATE].reshape(bsz, nc, q, g, SSD_STATE)
    cm = xbc[..., MIX_W + g * SSD_STATE:].reshape(bsz, nc, q, g, SSD_STATE)
    dt = jax.nn.softplus(dt_raw.astype(jnp.float32).reshape(bsz, nc, q, g, r)
                         + dt_bias.astype(jnp.float32).reshape(g, r))
    a = -jnp.exp(a_log.astype(jnp.float32)).reshape(g, r)
    a_cs = jnp.cumsum(dt * a, axis=2)
    xdt = xs * dt[..., None]
    seg = a_cs[:, :, :, None] - a_cs[:, :, None, :]
    decay = jnp.exp(jnp.where(_tril(q)[None, None, :, :, None, None], seg, -jnp.inf))
    scores = jnp.einsum('bcign,bcjgn->bcijg', cm, bm)
    y = jnp.einsum('bcijg,bcijgr,bcjgrp->bcigrp', scores, decay, xdt)
    decay_end = jnp.exp(a_cs[:, :, -1:] - a_cs)
    chunk_states = jnp.einsum('bcjgn,bcjgr,bcjgrp->bcgrpn', bm, decay_end, xdt)
    prev = _inter_chunk_states(chunk_states, jnp.exp(a_cs[:, :, -1])[..., None, None])
    y = y + jnp.einsum('bcign,bcgrpn,bcigr->bcigrp', cm, prev, jnp.exp(a_cs))
    y = y + xs * d_skip.astype(jnp.float32).reshape(g, r)[:, :, None]
    y = y.reshape(bsz, seq, MIX_W) * jax.nn.silu(z.astype(jnp.float32))
    y = _rms(y.reshape(bsz, seq, g, MIX_W // g), norm_w.reshape(g, MIX_W // g))
    return y.reshape(bsz, seq, MIX_W).astype(z.dtype)


def _complex_affine_combine(e1, e2):
    a1r, a1i, b1r, b1i = e1
    a2r, a2i, b2r, b2i = e2
    return (a1r * a2r - a1i * a2i, a1r * a2i + a1i * a2r,
            a2r * b1r - a2i * b1i + b2r, a2r * b1i + a2i * b1r + b2i)


def s5_mixer(u, lam_re, lam_im, b_re, b_im, c_re, c_im, d_skip, log_dt, glu_w, glu_b):
    bsz, seq, _ = u.shape
    f32 = jnp.float32
    uf = u.astype(f32).reshape(bsz, seq, S5_GROUPS, S5_GROUP_CH)
    lam_re, lam_im = lam_re.astype(f32), lam_im.astype(f32)
    dt = jnp.exp(log_dt.astype(f32))[:, None]
    mag = jnp.exp(lam_re * dt)
    lb_re, lb_im = mag * jnp.cos(lam_im * dt), mag * jnp.sin(lam_im * dt)
    den = lam_re * lam_re + lam_im * lam_im
    f_re = ((lb_re - 1.0) * lam_re + lb_im * lam_im) / den
    f_im = (lb_im * lam_re - (lb_re - 1.0) * lam_im) / den
    b_re, b_im = b_re.astype(f32), b_im.astype(f32)
    bb_re = f_re[..., None] * b_re - f_im[..., None] * b_im
    bb_im = f_re[..., None] * b_im + f_im[..., None] * b_re
    bu_re = jnp.einsum('blgh,gph->lbgp', uf, bb_re)
    bu_im = jnp.einsum('blgh,gph->lbgp', uf, bb_im)
    a_re = jnp.broadcast_to(lb_re, (seq, 1, S5_GROUPS, S5_STATE))
    a_im = jnp.broadcast_to(lb_im, (seq, 1, S5_GROUPS, S5_STATE))
    _, _, s_re, s_im = lax.associative_scan(_complex_affine_combine, (a_re, a_im, bu_re, bu_im), axis=0)
    y = (jnp.einsum('lbgp,ghp->blgh', s_re, c_re.astype(f32))
         - jnp.einsum('lbgp,ghp->blgh', s_im, c_im.astype(f32)))
    y = (y + d_skip.astype(f32).reshape(S5_GROUPS, S5_GROUP_CH) * uf).reshape(bsz, seq, MIX_W)
    y = jax.nn.gelu(y)
    out = y * jax.nn.sigmoid(y @ glu_w.astype(f32) + glu_b.astype(f32))
    return out.astype(u.dtype)


def gla_mixer(q, k, v, r, g_low, gate_w2, gate_b, norm_w):
    bsz, seq, _ = q.shape
    nc, c = seq // GLA_CHUNK, GLA_CHUNK
    f32 = jnp.float32
    qh = q.astype(f32).reshape(bsz, nc, c, GLA_HEADS, GLA_DK) * GLA_DK ** -0.5
    kh = k.astype(f32).reshape(bsz, nc, c, GLA_HEADS, GLA_DK)
    vh = v.astype(f32).reshape(bsz, nc, c, GLA_HEADS, GLA_DV)
    log_a = jax.nn.log_sigmoid(g_low.astype(f32) @ gate_w2.astype(f32) + gate_b.astype(f32)) / GLA_GATE_NORM
    b_cs = jnp.cumsum(log_a.reshape(bsz, nc, c, GLA_HEADS, GLA_DK), axis=2)
    diff = b_cs[:, :, :, None] - b_cs[:, :, None, :]
    dec = jnp.exp(jnp.where(_tril(c)[None, None, :, :, None, None], diff, -jnp.inf))
    scores = jnp.einsum('bcihd,bcjhd,bcijhd->bcijh', qh, kh, dec)
    o = jnp.einsum('bcijh,bcjhv->bcihv', scores, vh)
    k_end = kh * jnp.exp(b_cs[:, :, -1:] - b_cs)
    chunk_states = jnp.einsum('bcjhd,bcjhv->bchdv', k_end, vh)
    prev = _inter_chunk_states(chunk_states, jnp.exp(b_cs[:, :, -1])[..., None])
    o = o + jnp.einsum('bcihd,bchdv->bcihv', qh * jnp.exp(b_cs), prev)
    o = _rms(o.reshape(bsz, seq, GLA_HEADS, GLA_DV), norm_w)
    o = o * jax.nn.silu(r.astype(f32)).reshape(bsz, seq, GLA_HEADS, GLA_DV)
    return o.reshape(bsz, seq, MIX_W).astype(q.dtype)


def gdn_mixer(q, k, v, z, beta_raw, a_raw, conv_w, a_log, dt_bias, norm_w):
    bsz, seq, _ = q.shape
    nc, c = seq // GDN_CHUNK, GDN_CHUNK
    f32 = jnp.float32
    qkv = jax.nn.silu(causal_dwconv(jnp.concatenate([q, k, v], axis=-1), conv_w).astype(f32))
    qs, ks, vs = jnp.split(qkv, [GDN_HEADS * GDN_DK, 2 * GDN_HEADS * GDN_DK], axis=-1)

    def heads(t, d):
        return t.reshape(bsz, nc, c, GDN_HEADS, d).transpose(0, 1, 3, 2, 4)

    def per_head(t):
        return t.reshape(bsz, nc, c, GDN_HEADS).transpose(0, 1, 3, 2)

    qh = _l2norm(heads(qs, GDN_DK)) * GDN_DK ** -0.5
    kh = _l2norm(heads(ks, GDN_DK))
    vh = heads(vs, GDN_DV)
    beta = per_head(jax.nn.sigmoid(beta_raw.astype(f32)))
    g = per_head(-jnp.exp(a_log.astype(f32)) * jax.nn.softplus(a_raw.astype(f32) + dt_bias.astype(f32)))
    gc = jnp.cumsum(g, axis=-1)
    decay = jnp.exp(jnp.where(_tril(c), gc[..., :, None] - gc[..., None, :], -jnp.inf))
    kb = kh * beta[..., None]
    a_strict = jnp.where(_tril(c, strict=True), jnp.einsum('bnhid,bnhjd->bnhij', kb, kh) * decay, 0.0)
    u = lax.linalg.triangular_solve(a_strict, vh * beta[..., None], left_side=True, lower=True, unit_diagonal=True)
    w = lax.linalg.triangular_solve(a_strict, kb * jnp.exp(gc)[..., None], left_side=True, lower=True, unit_diagonal=True)
    q_dec = qh * jnp.exp(gc)[..., None]
    attn = jnp.einsum('bnhid,bnhjd->bnhij', qh, kh) * decay
    k_end = kh * jnp.exp(gc[..., -1:] - gc)[..., None]
    chunk_decay = jnp.exp(gc[..., -1])[..., None, None]

    def step(state, inp):
        u_c, w_c, qd_c, at_c, ke_c, cd_c = inp
        v_new = u_c - jnp.einsum('bhid,bhdv->bhiv', w_c, state)
        o_c = jnp.einsum('bhid,bhdv->bhiv', qd_c, state) + jnp.einsum('bhij,bhjv->bhiv', at_c, v_new)
        return state * cd_c + jnp.einsum('bhjd,bhjv->bhdv', ke_c, v_new), o_c

    init = jnp.zeros((bsz, GDN_HEADS, GDN_DK, GDN_DV), f32)
    scan_in = tuple(jnp.moveaxis(t, 1, 0) for t in (u, w, q_dec, attn, k_end, chunk_decay))
    _, o = lax.scan(step, init, scan_in)
    o = o.transpose(1, 0, 3, 2, 4).reshape(bsz, seq, GDN_HEADS, GDN_DV)
    o = _rms(o, norm_w) * jax.nn.silu(z.astype(f32)).reshape(bsz, seq, GDN_HEADS, GDN_DV)
    return o.reshape(bsz, seq, MIX_W).astype(q.dtype)


def conv_ffn(h, w_gate, w_up, conv_w, w_down):
    gate = causal_dwconv(h @ w_gate, conv_w)
    return (jax.nn.silu(gate) * (h @ w_up)) @ w_down


def setup_inputs(seed: int = 0) -> dict:
    key = jax.random.key(seed)
    keys = iter(jax.random.split(key, 48))

    def normal(shape, scale):
        return jax.random.normal(next(keys), shape, jnp.float32) * scale

    def uniform(shape, lo, hi):
        return jax.random.uniform(next(keys), shape, jnp.float32, lo, hi)

    def gain(width):
        return 1.0 + normal((DEPTH, width), 0.02)

    def dt_bias(n):
        dt = jnp.exp(uniform((DEPTH, n), math.log(1e-3), math.log(1e-1)))
        return dt + jnp.log(-jnp.expm1(-dt))

    mode_index = jnp.arange(S5_STATE, dtype=jnp.float32)
    return {
        'x': normal((BATCH, SEQ, D_MODEL), 1.0),
        'norm_mix_pre': gain(D_MODEL),
        'norm_mix_post': gain(D_MODEL),
        'norm_ffn_pre': gain(D_MODEL),
        'norm_ffn_post': gain(D_MODEL),
        'w_in': normal((DEPTH, D_MODEL, D_IN_PROJ), D_MODEL ** -0.5),
        'w_out': normal((DEPTH, D_MIX, D_MODEL), D_MIX ** -0.5),
        'ssd_conv_w': normal((DEPTH, SSD_CONV, SSD_XBC), SSD_CONV ** -0.5),
        'ssd_conv_b': normal((DEPTH, SSD_XBC), 0.01),
        'ssd_dt_bias': dt_bias(SSD_HEADS),
        'ssd_a_log': jnp.log(uniform((DEPTH, SSD_HEADS), 1.0, 16.0)),
        'ssd_d': 1.0 + normal((DEPTH, SSD_HEADS), 0.01),
        'ssd_norm': gain(MIX_W),
        's5_lambda_re': -0.5 + normal((DEPTH, S5_GROUPS, S5_STATE), 0.01),
        's5_lambda_im': math.pi * mode_index + normal((DEPTH, S5_GROUPS, S5_STATE), 0.01),
        's5_b_re': normal((DEPTH, S5_GROUPS, S5_STATE, S5_GROUP_CH), (2 * S5_GROUP_CH) ** -0.5),
        's5_b_im': normal((DEPTH, S5_GROUPS, S5_STATE, S5_GROUP_CH), (2 * S5_GROUP_CH) ** -0.5),
        's5_c_re': normal((DEPTH, S5_GROUPS, S5_GROUP_CH, S5_STATE), S5_STATE ** -0.5),
        's5_c_im': normal((DEPTH, S5_GROUPS, S5_GROUP_CH, S5_STATE), S5_STATE ** -0.5),
        's5_d': normal((DEPTH, MIX_W), 1.0),
        's5_log_dt': uniform((DEPTH, S5_GROUPS), math.log(S5_DT_MIN), math.log(S5_DT_MAX)),
        's5_glu_w': normal((DEPTH, MIX_W, MIX_W), MIX_W ** -0.5),
        's5_glu_b': normal((DEPTH, MIX_W), 0.01),
        'gla_gate_w2': normal((DEPTH, GLA_GATE_RANK, GLA_HEADS * GLA_DK), GLA_GATE_RANK ** -0.5),
        'gla_gate_b': normal((DEPTH, GLA_HEADS * GLA_DK), 0.01),
        'gla_norm': gain(GLA_DV),
        'gdn_conv_w': normal((DEPTH, GDN_CONV, 3 * MIX_W), GDN_CONV ** -0.5),
        'gdn_a_log': jnp.log(uniform((DEPTH, GDN_HEADS), 1.0, 16.0)),
        'gdn_dt_bias': dt_bias(GDN_HEADS),
        'gdn_norm': gain(GDN_DV),
        'ffn_w_gate': normal((DEPTH, D_MODEL, D_FF), D_MODEL ** -0.5),
        'ffn_w_up': normal((DEPTH, D_MODEL, D_FF), D_MODEL ** -0.5),
        'ffn_conv_w': normal((DEPTH, FFN_CONV, D_FF), FFN_CONV ** -0.5),
        'ffn_w_down': normal((DEPTH, D_FF, D_MODEL), D_FF ** -0.5),
    }


def reference(x, norm_mix_pre, norm_mix_post, norm_ffn_pre, norm_ffn_post, w_in, w_out,
              ssd_conv_w, ssd_conv_b, ssd_dt_bias, ssd_a_log, ssd_d, ssd_norm,
              s5_lambda_re, s5_lambda_im, s5_b_re, s5_b_im, s5_c_re, s5_c_im, s5_d, s5_log_dt,
              s5_glu_w, s5_glu_b, gla_gate_w2, gla_gate_b, gla_norm,
              gdn_conv_w, gdn_a_log, gdn_dt_bias, gdn_norm,
              ffn_w_gate, ffn_w_up, ffn_conv_w, ffn_w_down):
    for l in range(DEPTH):
        h = rmsnorm(x, norm_mix_pre[l])
        (ssd_z, ssd_xbc, ssd_dt, s5_u, gla_q, gla_k, gla_v, gla_r, gla_g,
         gdn_q, gdn_k, gdn_v, gdn_z, gdn_b, gdn_a) = _split_columns(h @ w_in[l])
        y_ssd = ssd_mixer(ssd_z, ssd_xbc, ssd_dt, ssd_conv_w[l], ssd_conv_b[l], ssd_dt_bias[l],
                          ssd_a_log[l], ssd_d[l], ssd_norm[l])
        y_s5 = s5_mixer(s5_u, s5_lambda_re[l], s5_lambda_im[l], s5_b_re[l], s5_b_im[l], s5_c_re[l],
                        s5_c_im[l], s5_d[l], s5_log_dt[l], s5_glu_w[l], s5_glu_b[l])
        y_gla = gla_mixer(gla_q, gla_k, gla_v, gla_r, gla_g, gla_gate_w2[l], gla_gate_b[l], gla_norm[l])
        y_gdn = gdn_mixer(gdn_q, gdn_k, gdn_v, gdn_z, gdn_b, gdn_a, gdn_conv_w[l], gdn_a_log[l],
                          gdn_dt_bias[l], gdn_norm[l])
        mixed = jnp.concatenate([y_ssd, y_s5, y_gla, y_gdn], axis=-1) @ w_out[l]
        x = x + rmsnorm(mixed, norm_mix_post[l])
        h = rmsnorm(x, norm_ffn_pre[l])
        x = x + rmsnorm(conv_ffn(h, ffn_w_gate[l], ffn_w_up[l], ffn_conv_w[l], ffn_w_down[l]), norm_ffn_post[l])
    return x
```

```python
import functools
import math

import jax
import jax.numpy as jnp
from jax import lax
from jax.experimental import pallas as pl
from jax.experimental.pallas import tpu as pltpu

D_MODEL = 4096
DEPTH = 2
MIX_W = D_MODEL // 4
RMS_EPS = 1e-6
D_FF = 11008
FFN_CONV = 3

SSD_HEAD_DIM = 64
SSD_HEADS = MIX_W // SSD_HEAD_DIM
SSD_STATE = 128
SSD_GROUPS = 2
SSD_CHUNK = 64
SSD_XBC = MIX_W + 2 * SSD_GROUPS * SSD_STATE

S5_GROUP_CH = 16
S5_GROUPS = MIX_W // S5_GROUP_CH
S5_STATE = 64

GLA_HEADS = 4
GLA_DK = MIX_W // (2 * GLA_HEADS)
GLA_DV = MIX_W // GLA_HEADS
GLA_GATE_RANK = 16
GLA_GATE_NORM = 16.0
GLA_CHUNK = 16

GDN_HEADS = 8
GDN_DK = MIX_W // GDN_HEADS
GDN_DV = MIX_W // GDN_HEADS
GDN_CHUNK = 64

SPLIT_SIZES = (
    MIX_W, SSD_XBC, SSD_HEADS,
    MIX_W,
    GLA_HEADS * GLA_DK, GLA_HEADS * GLA_DK, MIX_W, MIX_W, GLA_GATE_RANK,
    GDN_HEADS * GDN_DK, GDN_HEADS * GDN_DK, GDN_HEADS * GDN_DV, MIX_W, GDN_HEADS, GDN_HEADS,
)
D_IN_PROJ = sum(SPLIT_SIZES)

LANES = 128
VMEM_LIMIT_BYTES = 56 * 1024 * 1024

_WIDE = (0, 1, 3, 4, 5, 6, 7, 9, 10, 11, 12)
_NARROW = (2, 8, 13, 14)
N_PROJ_PAD = 11264
PROJ_TN = 1024


def _segment_offsets():
    src = [0]
    for s in SPLIT_SIZES:
        src.append(src[-1] + s)
    dst = {}
    off = 0
    for i in _WIDE + _NARROW:
        dst[i] = off
        off += SPLIT_SIZES[i]
    return src, dst, off


_SRC_OFF, _DST_OFF, _USED = _segment_offsets()


def _permute_w_in(w):
    parts = [w[:, _SRC_OFF[i]:_SRC_OFF[i + 1]] for i in _WIDE + _NARROW]
    parts.append(jnp.zeros((w.shape[0], N_PROJ_PAD - _USED), w.dtype))
    return jnp.concatenate(parts, axis=1).astype(jnp.bfloat16)


def _seg(proj, i):
    return proj[..., _DST_OFF[i]:_DST_OFF[i] + SPLIT_SIZES[i]]


def _matmul_kernel(a_ref, w_ref, o_ref):
    o_ref[...] = jnp.dot(a_ref[...], w_ref[...], preferred_element_type=jnp.float32).astype(o_ref.dtype)


def _matmul(a, w, out_dtype, tm, tn):
    m, k = a.shape
    n = w.shape[1]
    assert m % tm == 0 and n % tn == 0
    return pl.pallas_call(
        _matmul_kernel,
        grid=(m // tm, n // tn),
        in_specs=[pl.BlockSpec((tm, k), lambda i, j: (i, 0)),
                  pl.BlockSpec((k, tn), lambda i, j: (0, j))],
        out_specs=pl.BlockSpec((tm, tn), lambda i, j: (i, j)),
        out_shape=jax.ShapeDtypeStruct((m, n), out_dtype),
        compiler_params=pltpu.CompilerParams(
            dimension_semantics=("parallel", "arbitrary"), vmem_limit_bytes=VMEM_LIMIT_BYTES),
        name="matmul",
    )(a, w)


def _matmul_ksplit_kernel(a_ref, w_ref, o_ref, acc_ref):
    k = pl.program_id(2)
    part = jnp.dot(a_ref[...], w_ref[...], preferred_element_type=jnp.float32)

    @pl.when(k == 0)
    def _():
        acc_ref[...] = part

    @pl.when(k != 0)
    def _():
        acc_ref[...] += part

    @pl.when(k == pl.num_programs(2) - 1)
    def _():
        o_ref[...] = acc_ref[...].astype(o_ref.dtype)


def _matmul_ksplit(a, w, out_dtype, tm, tn, tk):
    m, k = a.shape
    n = w.shape[1]
    assert m % tm == 0 and n % tn == 0 and k % tk == 0
    return pl.pallas_call(
        _matmul_ksplit_kernel,
        grid=(m // tm, n // tn, k // tk),
        in_specs=[pl.BlockSpec((tm, tk), lambda i, j, kk: (i, kk)),
                  pl.BlockSpec((tk, tn), lambda i, j, kk: (kk, j))],
        out_specs=pl.BlockSpec((tm, tn), lambda i, j, kk: (i, j)),
        out_shape=jax.ShapeDtypeStruct((m, n), out_dtype),
        scratch_shapes=[pltpu.VMEM((tm, tn), jnp.float32)],
        compiler_params=pltpu.CompilerParams(
            dimension_semantics=("parallel", "arbitrary", "arbitrary"), vmem_limit_bytes=VMEM_LIMIT_BYTES),
        name="matmul_ksplit",
    )(a, w)


def _prenorm_kernel(x_ref, w_ref, o_ref):
    x = x_ref[...]
    ms = jnp.mean(x * x, axis=-1, keepdims=True)
    o_ref[...] = (x * lax.rsqrt(ms + RMS_EPS) * w_ref[...]).astype(o_ref.dtype)


def _prenorm(x, w, tm=256):
    m, d = x.shape
    return pl.pallas_call(
        _prenorm_kernel,
        grid=(m // tm,),
        in_specs=[pl.BlockSpec((tm, d), lambda i: (i, 0)),
                  pl.BlockSpec((1, d), lambda i: (0, 0))],
        out_specs=pl.BlockSpec((tm, d), lambda i: (i, 0)),
        out_shape=jax.ShapeDtypeStruct((m, d), jnp.bfloat16),
        compiler_params=pltpu.CompilerParams(dimension_semantics=("parallel",)),
        name="prenorm",
    )(x, w.reshape(1, d))


def _postnorm_residual_kernel(x_ref, y_ref, w_ref, o_ref):
    y = y_ref[...]
    ms = jnp.mean(y * y, axis=-1, keepdims=True)
    o_ref[...] = x_ref[...] + y * lax.rsqrt(ms + RMS_EPS) * w_ref[...]


def _postnorm_residual(x, y, w, tm=256):
    m, d = x.shape
    return pl.pallas_call(
        _postnorm_residual_kernel,
        grid=(m // tm,),
        in_specs=[pl.BlockSpec((tm, d), lambda i: (i, 0)),
                  pl.BlockSpec((tm, d), lambda i: (i, 0)),
                  pl.BlockSpec((1, d), lambda i: (0, 0))],
        out_specs=pl.BlockSpec((tm, d), lambda i: (i, 0)),
        out_shape=jax.ShapeDtypeStruct((m, d), jnp.float32),
        compiler_params=pltpu.CompilerParams(dimension_semantics=("parallel",)),
        name="postnorm_residual",
    )(x, y, w.reshape(1, d))


FFN_HALO = 16


def _ffn_gate_up_kernel(blocks_per_seq, h_ref, halo_ref, wg_ref, wu_ref, cw_ref, o_ref, g_ref):
    i = pl.program_id(0)
    tm = h_ref.shape[0]
    h = h_ref[...]
    wg = wg_ref[...]
    g = jnp.dot(h, wg, preferred_element_type=jnp.float32)
    g_halo = jnp.dot(halo_ref[...], wg, preferred_element_type=jnp.float32)
    g_halo = jnp.where(i % blocks_per_seq == 0, 0.0, g_halo)
    g_ref[0:FFN_HALO, :] = g_halo
    g_ref[FFN_HALO:FFN_HALO + tm, :] = g
    cw = cw_ref[...]
    conv = (cw[2:3, :] * g
            + cw[1:2, :] * g_ref[FFN_HALO - 1:FFN_HALO - 1 + tm, :]
            + cw[0:1, :] * g_ref[FFN_HALO - 2:FFN_HALO - 2 + tm, :])
    u = jnp.dot(h, wu_ref[...], preferred_element_type=jnp.float32)
    o_ref[...] = (conv * jax.nn.sigmoid(conv) * u).astype(o_ref.dtype)


def _ffn_gate_up(h, wg, wu, cw, seq, tm=1024, tf=256):
    m, d = h.shape
    f = wg.shape[1]
    assert m % tm == 0 and f % tf == 0 and seq % tm == 0 and tm % FFN_HALO == 0
    halo_blocks = tm // FFN_HALO
    return pl.pallas_call(
        functools.partial(_ffn_gate_up_kernel, seq // tm),
        grid=(m // tm, f // tf),
        in_specs=[pl.BlockSpec((tm, d), lambda i, j: (i, 0)),
                  pl.BlockSpec((FFN_HALO, d), lambda i, j: (jnp.maximum(i * halo_blocks - 1, 0), 0)),
                  pl.BlockSpec((d, tf), lambda i, j: (0, j)),
                  pl.BlockSpec((d, tf), lambda i, j: (0, j)),
                  pl.BlockSpec((FFN_CONV, tf), lambda i, j: (0, j))],
        out_specs=pl.BlockSpec((tm, tf), lambda i, j: (i, j)),
        out_shape=jax.ShapeDtypeStruct((m, f), jnp.bfloat16),
        scratch_shapes=[pltpu.VMEM((tm + FFN_HALO, tf), jnp.float32)],
        compiler_params=pltpu.CompilerParams(
            dimension_semantics=("parallel", "arbitrary"), vmem_limit_bytes=VMEM_LIMIT_BYTES),
        name="ffn_gate_up",
    )(h, h, wg, wu, cw)


def _rms(x, w):
    xf = x.astype(jnp.float32)
    return xf * lax.rsqrt(jnp.mean(xf * xf, axis=-1, keepdims=True) + RMS_EPS) * w.astype(jnp.float32)


def _l2norm(x):
    return x * lax.rsqrt(jnp.sum(x * x, axis=-1, keepdims=True) + 1e-6)


def _tril(n, strict=False):
    return jnp.tril(jnp.ones((n, n), dtype=bool), k=-1 if strict else 0)


def _causal_dwconv(x, w):
    k = w.shape[0]
    return lax.conv_general_dilated(
        x, w[:, None, :].astype(x.dtype), window_strides=(1,), padding=((k - 1, 0),),
        dimension_numbers=('NWC', 'WIO', 'NWC'), feature_group_count=x.shape[-1])


def _inter_chunk_states(chunk_states, chunk_decay):
    def step(state, inp):
        s_c, d_c = inp
        return state * d_c + s_c, state
    init = jnp.zeros_like(chunk_states[:, 0])
    _, prev = lax.scan(step, init, (jnp.moveaxis(chunk_states, 1, 0), jnp.moveaxis(chunk_decay, 1, 0)))
    return jnp.moveaxis(prev, 0, 1)


def _ssd_mixer(z, xbc, dt_raw, conv_w, conv_b, dt_bias, a_log, d_skip, norm_w):
    bsz, seq, _ = z.shape
    nc, q = seq // SSD_CHUNK, SSD_CHUNK
    g, r = SSD_GROUPS, SSD_HEADS // SSD_GROUPS
    xbc = jax.nn.silu((_causal_dwconv(xbc, conv_w) + conv_b).astype(jnp.float32))
    xs = xbc[..., :MIX_W].reshape(bsz, nc, q, g, r, SSD_HEAD_DIM)
    bm = xbc[..., MIX_W:MIX_W + g * SSD_STATE].reshape(bsz, nc, q, g, SSD_STATE)
    cm = xbc[..., MIX_W + g * SSD_STATE:].reshape(bsz, nc, q, g, SSD_STATE)
    dt = jax.nn.softplus(dt_raw.astype(jnp.float32).reshape(bsz, nc, q, g, r)
                         + dt_bias.astype(jnp.float32).reshape(g, r))
    a = -jnp.exp(a_log.astype(jnp.float32)).reshape(g, r)
    a_cs = jnp.cumsum(dt * a, axis=2)
    xdt = xs * dt[..., None]
    seg = a_cs[:, :, :, None] - a_cs[:, :, None, :]
    decay = jnp.exp(jnp.where(_tril(q)[None, None, :, :, None, None], seg, -jnp.inf))
    scores = jnp.einsum('bcign,bcjgn->bcijg', cm, bm)
    y = jnp.einsum('bcijg,bcijgr,bcjgrp->bcigrp', scores, decay, xdt)
    decay_end = jnp.exp(a_cs[:, :, -1:] - a_cs)
    chunk_states = jnp.einsum('bcjgn,bcjgr,bcjgrp->bcgrpn', bm, decay_end, xdt)
    prev = _inter_chunk_states(chunk_states, jnp.exp(a_cs[:, :, -1])[..., None, None])
    y = y + jnp.einsum('bcign,bcgrpn,bcigr->bcigrp', cm, prev, jnp.exp(a_cs))
    y = y + xs * d_skip.astype(jnp.float32).reshape(g, r)[:, :, None]
    y = y.reshape(bsz, seq, MIX_W) * jax.nn.silu(z.astype(jnp.float32))
    y = _rms(y.reshape(bsz, seq, g, MIX_W // g), norm_w.reshape(g, MIX_W // g))
    return y.reshape(bsz, seq, MIX_W).astype(z.dtype)


def _complex_affine_combine(e1, e2):
    a1r, a1i, b1r, b1i = e1
    a2r, a2i, b2r, b2i = e2
    return (a1r * a2r - a1i * a2i, a1r * a2i + a1i * a2r,
            a2r * b1r - a2i * b1i + b2r, a2r * b1i + a2i * b1r + b2i)


def _s5_mixer(u, lam_re, lam_im, b_re, b_im, c_re, c_im, d_skip, log_dt, glu_w, glu_b):
    bsz, seq, _ = u.shape
    f32 = jnp.float32
    uf = u.astype(f32).reshape(bsz, seq, S5_GROUPS, S5_GROUP_CH)
    dt = jnp.exp(log_dt.astype(f32))[:, None]
    mag = jnp.exp(lam_re * dt)
    lb_re, lb_im = mag * jnp.cos(lam_im * dt), mag * jnp.sin(lam_im * dt)
    den = lam_re * lam_re + lam_im * lam_im
    f_re = ((lb_re - 1.0) * lam_re + lb_im * lam_im) / den
    f_im = (lb_im * lam_re - (lb_re - 1.0) * lam_im) / den
    bb_re = f_re[..., None] * b_re - f_im[..., None] * b_im
    bb_im = f_re[..., None] * b_im + f_im[..., None] * b_re
    bu_re = jnp.einsum('blgh,gph->lbgp', uf, bb_re)
    bu_im = jnp.einsum('blgh,gph->lbgp', uf, bb_im)
    a_re = jnp.broadcast_to(lb_re, (seq, 1, S5_GROUPS, S5_STATE))
    a_im = jnp.broadcast_to(lb_im, (seq, 1, S5_GROUPS, S5_STATE))
    _, _, s_re, s_im = lax.associative_scan(_complex_affine_combine, (a_re, a_im, bu_re, bu_im), axis=0)
    y = (jnp.einsum('lbgp,ghp->blgh', s_re, c_re.astype(f32))
         - jnp.einsum('lbgp,ghp->blgh', s_im, c_im.astype(f32)))
    y = (y + d_skip.astype(f32).reshape(S5_GROUPS, S5_GROUP_CH) * uf).reshape(bsz, seq, MIX_W)
    y = jax.nn.gelu(y)
    out = y * jax.nn.sigmoid(y @ glu_w.astype(f32) + glu_b.astype(f32))
    return out.astype(u.dtype)


def _gla_mixer(q, k, v, r, g_low, gate_w2, gate_b, norm_w):
    bsz, seq, _ = q.shape
    nc, c = seq // GLA_CHUNK, GLA_CHUNK
    f32 = jnp.float32
    qh = q.astype(f32).reshape(bsz, nc, c, GLA_HEADS, GLA_DK) * GLA_DK ** -0.5
    kh = k.astype(f32).reshape(bsz, nc, c, GLA_HEADS, GLA_DK)
    vh = v.astype(f32).reshape(bsz, nc, c, GLA_HEADS, GLA_DV)
    log_a = jax.nn.log_sigmoid(g_low.astype(f32) @ gate_w2.astype(f32) + gate_b.astype(f32)) / GLA_GATE_NORM
    b_cs = jnp.cumsum(log_a.reshape(bsz, nc, c, GLA_HEADS, GLA_DK), axis=2)
    diff = b_cs[:, :, :, None] - b_cs[:, :, None, :]
    dec = jnp.exp(jnp.where(_tril(c)[None, None, :, :, None, None], diff, -jnp.inf))
    scores = jnp.einsum('bcihd,bcjhd,bcijhd->bcijh', qh, kh, dec)
    o = jnp.einsum('bcijh,bcjhv->bcihv', scores, vh)
    k_end = kh * jnp.exp(b_cs[:, :, -1:] - b_cs)
    chunk_states = jnp.einsum('bcjhd,bcjhv->bchdv', k_end, vh)
    prev = _inter_chunk_states(chunk_states, jnp.exp(b_cs[:, :, -1])[..., None])
    o = o + jnp.einsum('bcihd,bchdv->bcihv', qh * jnp.exp(b_cs), prev)
    o = _rms(o.reshape(bsz, seq, GLA_HEADS, GLA_DV), norm_w)
    o = o * jax.nn.silu(r.astype(f32)).reshape(bsz, seq, GLA_HEADS, GLA_DV)
    return o.reshape(bsz, seq, MIX_W).astype(q.dtype)


def _gdn_mixer(q, k, v, z, beta_raw, a_raw, conv_w, a_log, dt_bias, norm_w):
    bsz, seq, _ = q.shape
    nc, c = seq // GDN_CHUNK, GDN_CHUNK
    f32 = jnp.float32
    qkv = jax.nn.silu(_causal_dwconv(jnp.concatenate([q, k, v], axis=-1), conv_w).astype(f32))
    qs, ks, vs = jnp.split(qkv, [GDN_HEADS * GDN_DK, 2 * GDN_HEADS * GDN_DK], axis=-1)

    def heads(t, d):
        return t.reshape(bsz, nc, c, GDN_HEADS, d).transpose(0, 1, 3, 2, 4)

    def per_head(t):
        return t.reshape(bsz, nc, c, GDN_HEADS).transpose(0, 1, 3, 2)

    qh = _l2norm(heads(qs, GDN_DK)) * GDN_DK ** -0.5
    kh = _l2norm(heads(ks, GDN_DK))
    vh = heads(vs, GDN_DV)
    beta = per_head(jax.nn.sigmoid(beta_raw.astype(f32)))
    g = per_head(-jnp.exp(a_log.astype(f32)) * jax.nn.softplus(a_raw.astype(f32) + dt_bias.astype(f32)))
    gc = jnp.cumsum(g, axis=-1)
    decay = jnp.exp(jnp.where(_tril(c), gc[..., :, None] - gc[..., None, :], -jnp.inf))
    kb = kh * beta[..., None]
    a_strict = jnp.where(_tril(c, strict=True), jnp.einsum('bnhid,bnhjd->bnhij', kb, kh) * decay, 0.0)
    u = lax.linalg.triangular_solve(a_strict, vh * beta[..., None], left_side=True, lower=True, unit_diagonal=True)
    w = lax.linalg.triangular_solve(a_strict, kb * jnp.exp(gc)[..., None], left_side=True, lower=True, unit_diagonal=True)
    q_dec = qh * jnp.exp(gc)[..., None]
    attn = jnp.einsum('bnhid,bnhjd->bnhij', qh, kh) * decay
    k_end = kh * jnp.exp(gc[..., -1:] - gc)[..., None]
    chunk_decay = jnp.exp(gc[..., -1])[..., None, None]

    def step(state, inp):
        u_c, w_c, qd_c, at_c, ke_c, cd_c = inp
        v_new = u_c - jnp.einsum('bhid,bhdv->bhiv', w_c, state)
        o_c = jnp.einsum('bhid,bhdv->bhiv', qd_c, state) + jnp.einsum('bhij,bhjv->bhiv', at_c, v_new)
        return state * cd_c + jnp.einsum('bhjd,bhjv->bhdv', ke_c, v_new), o_c

    init = jnp.zeros((bsz, GDN_HEADS, GDN_DK, GDN_DV), f32)
    scan_in = tuple(jnp.moveaxis(t, 1, 0) for t in (u, w, q_dec, attn, k_end, chunk_decay))
    _, o = lax.scan(step, init, scan_in)
    o = o.transpose(1, 0, 3, 2, 4).reshape(bsz, seq, GDN_HEADS, GDN_DV)
    o = _rms(o, norm_w) * jax.nn.silu(z.astype(f32)).reshape(bsz, seq, GDN_HEADS, GDN_DV)
    return o.reshape(bsz, seq, MIX_W).astype(q.dtype)


def kernel(x, norm_mix_pre, norm_mix_post, norm_ffn_pre, norm_ffn_post, w_in, w_out, ssd_conv_w, ssd_conv_b, ssd_dt_bias, ssd_a_log, ssd_d, ssd_norm, s5_lambda_re, s5_lambda_im, s5_b_re, s5_b_im, s5_c_re, s5_c_im, s5_d, s5_log_dt, s5_glu_w, s5_glu_b, gla_gate_w2, gla_gate_b, gla_norm, gdn_conv_w, gdn_a_log, gdn_dt_bias, gdn_norm, ffn_w_gate, ffn_w_up, ffn_conv_w, ffn_w_down):
    bsz, seq, d = x.shape
    m = bsz * seq
    xf = x.reshape(m, d)
    bf16 = jnp.bfloat16
    for l in range(DEPTH):
        h = _prenorm(xf, norm_mix_pre[l])
        proj = _matmul(h, _permute_w_in(w_in[l]), jnp.float32, 1024, PROJ_TN).reshape(bsz, seq, N_PROJ_PAD)
        (ssd_z, ssd_xbc, ssd_dt, s5_u, gla_q, gla_k, gla_v, gla_r, gla_g,
         gdn_q, gdn_k, gdn_v, gdn_z, gdn_b, gdn_a) = [_seg(proj, i) for i in range(len(SPLIT_SIZES))]
        y_ssd = _ssd_mixer(ssd_z, ssd_xbc, ssd_dt, ssd_conv_w[l], ssd_conv_b[l], ssd_dt_bias[l],
                           ssd_a_log[l], ssd_d[l], ssd_norm[l])
        y_s5 = _s5_mixer(s5_u, s5_lambda_re[l], s5_lambda_im[l], s5_b_re[l], s5_b_im[l], s5_c_re[l],
                         s5_c_im[l], s5_d[l], s5_log_dt[l], s5_glu_w[l], s5_glu_b[l])
        y_gla = _gla_mixer(gla_q, gla_k, gla_v, gla_r, gla_g, gla_gate_w2[l], gla_gate_b[l], gla_norm[l])
        y_gdn = _gdn_mixer(gdn_q, gdn_k, gdn_v, gdn_z, gdn_b, gdn_a, gdn_conv_w[l], gdn_a_log[l],
                           gdn_dt_bias[l], gdn_norm[l])
        cat = jnp.concatenate([y_ssd, y_s5, y_gla, y_gdn], axis=-1).reshape(m, d).astype(bf16)
        mixed = _matmul(cat, w_out[l].astype(bf16), jnp.float32, 1024, 1024)
        xf = _postnorm_residual(xf, mixed, norm_mix_post[l])

        h = _prenorm(xf, norm_ffn_pre[l])
        act = _ffn_gate_up(h, ffn_w_gate[l].astype(bf16), ffn_w_up[l].astype(bf16), ffn_conv_w[l], seq)
        down = _matmul_ksplit(act, ffn_w_down[l].astype(bf16), jnp.float32, 512, 1024, D_FF // 2)
        xf = _postnorm_residual(xf, down, norm_ffn_post[l])
    return xf.reshape(bsz, seq, d)
```

```python
import functools
import math

import jax
import jax.numpy as jnp
from jax import lax
from jax.experimental import pallas as pl
from jax.experimental.pallas import tpu as pltpu

D_MODEL = 4096
DEPTH = 2
MIX_W = D_MODEL // 4
RMS_EPS = 1e-6
D_FF = 11008
FFN_CONV = 3

SSD_HEAD_DIM = 64
SSD_HEADS = MIX_W // SSD_HEAD_DIM
SSD_STATE = 128
SSD_GROUPS = 2
SSD_CONV = 4
SSD_XBC = MIX_W + 2 * SSD_GROUPS * SSD_STATE
SSD_Q = 128

S5_GROUP_CH = 16
S5_GROUPS = MIX_W // S5_GROUP_CH
S5_STATE = 64
S5_T = 16

GLA_HEADS = 4
GLA_DK = MIX_W // (2 * GLA_HEADS)
GLA_DV = MIX_W // GLA_HEADS
GLA_GATE_RANK = 16
GLA_GATE_NORM = 16.0
GLA_Q = 64

GDN_HEADS = 8
GDN_DK = MIX_W // GDN_HEADS
GDN_DV = MIX_W // GDN_HEADS
GDN_CONV = 4
GDN_C = 64

SPLIT_SIZES = (
    MIX_W, SSD_XBC, SSD_HEADS,
    MIX_W,
    GLA_HEADS * GLA_DK, GLA_HEADS * GLA_DK, MIX_W, MIX_W, GLA_GATE_RANK,
    GDN_HEADS * GDN_DK, GDN_HEADS * GDN_DK, GDN_HEADS * GDN_DV, MIX_W, GDN_HEADS, GDN_HEADS,
)
D_IN_PROJ = sum(SPLIT_SIZES)
(SEG_SSD_Z, SEG_SSD_XBC, SEG_SSD_DT, SEG_S5_U, SEG_GLA_Q, SEG_GLA_K, SEG_GLA_V, SEG_GLA_R, SEG_GLA_G,
 SEG_GDN_Q, SEG_GDN_K, SEG_GDN_V, SEG_GDN_Z, SEG_GDN_B, SEG_GDN_A) = range(len(SPLIT_SIZES))

LANES = 128
SUBLANES = 8
VMEM_LIMIT_BYTES = 56 * 1024 * 1024

N_PROJ_PAD = 11264
PROJ_TN = 1024
COL_GDN_QKV = 0
COL_SSD_XBC = 3072
COL_NARROW = 4608
COL_GLA_QK = 5120
COL_GLA_V = 6144
COL_GLA_R = 7168
COL_GDN_Z = 8192
COL_SSD_Z = 9216
COL_S5_U = 10240
LANE_SSD_DT = 0
LANE_GLA_G = 16
LANE_GDN_B = 32
LANE_GDN_A = 40
_DST_OFF = {
    SEG_GDN_Q: COL_GDN_QKV, SEG_GDN_K: COL_GDN_QKV + 1024, SEG_GDN_V: COL_GDN_QKV + 2048,
    SEG_SSD_XBC: COL_SSD_XBC,
    SEG_SSD_DT: COL_NARROW + LANE_SSD_DT, SEG_GLA_G: COL_NARROW + LANE_GLA_G,
    SEG_GDN_B: COL_NARROW + LANE_GDN_B, SEG_GDN_A: COL_NARROW + LANE_GDN_A,
    SEG_GLA_Q: COL_GLA_QK, SEG_GLA_K: COL_GLA_QK + 512, SEG_GLA_V: COL_GLA_V, SEG_GLA_R: COL_GLA_R,
    SEG_GDN_Z: COL_GDN_Z, SEG_SSD_Z: COL_SSD_Z, SEG_S5_U: COL_S5_U,
}

f32 = jnp.float32
bf16 = jnp.bfloat16


def _permute_w_in(w):
    src = [0]
    for s in SPLIT_SIZES:
        src.append(src[-1] + s)
    order = sorted(_DST_OFF, key=_DST_OFF.get)
    parts, pos = [], 0
    for i in order:
        if _DST_OFF[i] > pos:
            parts.append(jnp.zeros((w.shape[0], _DST_OFF[i] - pos), w.dtype))
        parts.append(w[:, src[i]:src[i + 1]])
        pos = _DST_OFF[i] + SPLIT_SIZES[i]
    parts.append(jnp.zeros((w.shape[0], N_PROJ_PAD - pos), w.dtype))
    return jnp.concatenate(parts, axis=1).astype(bf16)


def _pad_lanes(vec, offset):
    return jnp.zeros((1, LANES), f32).at[0, offset:offset + vec.shape[0]].set(vec.astype(f32))


def _dot(a, b):
    return jnp.dot(a, b, preferred_element_type=f32)


def _dot_nt(a, b):
    return lax.dot_general(a, b, (((1,), (1,)), ((), ())), preferred_element_type=f32)


def _dot_tn(a, b):
    return lax.dot_general(a, b, (((0,), (0,)), ((), ())), preferred_element_type=f32)


def _split3(x):
    hi = x.astype(bf16)
    r1 = x - hi.astype(f32)
    mid = r1.astype(bf16)
    lo = (r1 - mid.astype(f32)).astype(bf16)
    return hi, mid, lo


def _sel_dot_l(sel, x):
    hi, mid, lo = _split3(x)
    return _dot(sel, hi) + _dot(sel, mid) + _dot(sel, lo)


def _sel_dot_r(x, sel):
    hi, mid, lo = _split3(x)
    return _dot(hi, sel) + _dot(mid, sel) + _dot(lo, sel)


def _dot3(a, b):
    a1 = a.astype(bf16)
    a2 = (a - a1.astype(f32)).astype(bf16)
    b1 = b.astype(bf16)
    b2 = (b - b1.astype(f32)).astype(bf16)
    return _dot(a1, b1) + _dot(a1, b2) + _dot(a2, b1)


def _softplus(x):
    return jnp.maximum(x, 0.0) + jnp.log1p(jnp.exp(-jnp.abs(x)))


def _silu(x):
    return x * jax.nn.sigmoid(x)


def _iota2(shape, dim):
    return lax.broadcasted_iota(jnp.int32, shape, dim)


def _tri_ones(n):
    return (_iota2((n, n), 0) >= _iota2((n, n), 1)).astype(bf16)


def _causal_conv(x, buf_ref, cw, kw):
    q = x.shape[0]
    buf_ref[SUBLANES:SUBLANES + q, :] = x
    acc = cw[kw - 1:kw, :] * x
    for k in range(kw - 1):
        s = SUBLANES - (kw - 1) + k
        acc = acc + cw[k:k + 1, :] * buf_ref[s:s + q, :]
    buf_ref[0:SUBLANES, :] = x[q - SUBLANES:q, :]
    return acc


def _group_rms(y, nw):
    ms = jnp.mean(y * y, axis=-1, keepdims=True)
    return y * lax.rsqrt(ms + RMS_EPS) * nw


def _matmul_kernel(a_ref, w_ref, o_ref):
    o_ref[...] = _dot(a_ref[...], w_ref[...]).astype(o_ref.dtype)


def _matmul(a, w, out_dtype, tm, tn):
    m, k = a.shape
    n = w.shape[1]
    assert m % tm == 0 and n % tn == 0
    return pl.pallas_call(
        _matmul_kernel,
        grid=(m // tm, n // tn),
        in_specs=[pl.BlockSpec((tm, k), lambda i, j: (i, 0)),
                  pl.BlockSpec((k, tn), lambda i, j: (0, j))],
        out_specs=pl.BlockSpec((tm, tn), lambda i, j: (i, j)),
        out_shape=jax.ShapeDtypeStruct((m, n), out_dtype),
        compiler_params=pltpu.CompilerParams(
            dimension_semantics=("parallel", "arbitrary"), vmem_limit_bytes=VMEM_LIMIT_BYTES),
        name="matmul",
    )(a, w)


def _outproj_kernel(a0_ref, a1_ref, a2_ref, a3_ref, w_ref, o_ref):
    acc = _dot(a0_ref[...], w_ref[0:MIX_W, :])
    acc = acc + _dot(a1_ref[...], w_ref[MIX_W:2 * MIX_W, :])
    acc = acc + _dot(a2_ref[...], w_ref[2 * MIX_W:3 * MIX_W, :])
    acc = acc + _dot(a3_ref[...], w_ref[3 * MIX_W:4 * MIX_W, :])
    o_ref[...] = acc


def _outproj(parts, w, tm=1024, tn=1024):
    m = parts[0].shape[0]
    k, n = w.shape
    assert m % tm == 0 and n % tn == 0 and k == 4 * MIX_W
    a_spec = pl.BlockSpec((tm, MIX_W), lambda i, j: (i, 0))
    return pl.pallas_call(
        _outproj_kernel,
        grid=(m // tm, n // tn),
        in_specs=[a_spec, a_spec, a_spec, a_spec, pl.BlockSpec((k, tn), lambda i, j: (0, j))],
        out_specs=pl.BlockSpec((tm, tn), lambda i, j: (i, j)),
        out_shape=jax.ShapeDtypeStruct((m, n), f32),
        compiler_params=pltpu.CompilerParams(
            dimension_semantics=("parallel", "arbitrary"), vmem_limit_bytes=VMEM_LIMIT_BYTES),
        name="outproj",
    )(*parts, w)


def _matmul_ksplit_kernel(a_ref, w_ref, o_ref, acc_ref):
    k = pl.program_id(2)
    part = _dot(a_ref[...], w_ref[...])

    @pl.when(k == 0)
    def _():
        acc_ref[...] = part

    @pl.when(k != 0)
    def _():
        acc_ref[...] += part

    @pl.when(k == pl.num_programs(2) - 1)
    def _():
        o_ref[...] = acc_ref[...].astype(o_ref.dtype)


def _matmul_ksplit(a, w, out_dtype, tm, tn, tk):
    m, k = a.shape
    n = w.shape[1]
    assert m % tm == 0 and n % tn == 0 and k % tk == 0
    return pl.pallas_call(
        _matmul_ksplit_kernel,
        grid=(m // tm, n // tn, k // tk),
        in_specs=[pl.BlockSpec((tm, tk), lambda i, j, kk: (i, kk)),
                  pl.BlockSpec((tk, tn), lambda i, j, kk: (kk, j))],
        out_specs=pl.BlockSpec((tm, tn), lambda i, j, kk: (i, j)),
        out_shape=jax.ShapeDtypeStruct((m, n), out_dtype),
        scratch_shapes=[pltpu.VMEM((tm, tn), f32)],
        compiler_params=pltpu.CompilerParams(
            dimension_semantics=("parallel", "arbitrary", "arbitrary"), vmem_limit_bytes=VMEM_LIMIT_BYTES),
        name="matmul_ksplit",
    )(a, w)


def _prenorm_kernel(x_ref, w_ref, o_ref):
    x = x_ref[...]
    ms = jnp.mean(x * x, axis=-1, keepdims=True)
    o_ref[...] = (x * lax.rsqrt(ms + RMS_EPS) * w_ref[...]).astype(o_ref.dtype)


def _prenorm(x, w, tm=256):
    m, d = x.shape
    assert m % tm == 0
    return pl.pallas_call(
        _prenorm_kernel,
        grid=(m // tm,),
        in_specs=[pl.BlockSpec((tm, d), lambda i: (i, 0)),
                  pl.BlockSpec((1, d), lambda i: (0, 0))],
        out_specs=pl.BlockSpec((tm, d), lambda i: (i, 0)),
        out_shape=jax.ShapeDtypeStruct((m, d), bf16),
        compiler_params=pltpu.CompilerParams(dimension_semantics=("parallel",)),
        name="prenorm",
    )(x, w.reshape(1, d))


def _postnorm_residual_kernel(x_ref, y_ref, w_ref, o_ref):
    y = y_ref[...]
    ms = jnp.mean(y * y, axis=-1, keepdims=True)
    o_ref[...] = x_ref[...] + y * lax.rsqrt(ms + RMS_EPS) * w_ref[...]


def _postnorm_residual(x, y, w, tm=256):
    m, d = x.shape
    assert m % tm == 0
    return pl.pallas_call(
        _postnorm_residual_kernel,
        grid=(m // tm,),
        in_specs=[pl.BlockSpec((tm, d), lambda i: (i, 0)),
                  pl.BlockSpec((tm, d), lambda i: (i, 0)),
                  pl.BlockSpec((1, d), lambda i: (0, 0))],
        out_specs=pl.BlockSpec((tm, d), lambda i: (i, 0)),
        out_shape=jax.ShapeDtypeStruct((m, d), f32),
        compiler_params=pltpu.CompilerParams(dimension_semantics=("parallel",)),
        name="postnorm_residual",
    )(x, y, w.reshape(1, d))


FFN_HALO = 16


def _ffn_gate_up_kernel(blocks_per_seq, h_ref, halo_ref, wg_ref, wu_ref, cw_ref, o_ref, g_ref):
    i = pl.program_id(0)
    tm = h_ref.shape[0]
    h = h_ref[...]
    wg = wg_ref[...]
    g = _dot(h, wg)
    g_halo = _dot(halo_ref[...], wg)
    g_halo = jnp.where(i % blocks_per_seq == 0, 0.0, g_halo)
    g_ref[0:FFN_HALO, :] = g_halo
    g_ref[FFN_HALO:FFN_HALO + tm, :] = g
    cw = cw_ref[...]
    conv = (cw[2:3, :] * g
            + cw[1:2, :] * g_ref[FFN_HALO - 1:FFN_HALO - 1 + tm, :]
            + cw[0:1, :] * g_ref[FFN_HALO - 2:FFN_HALO - 2 + tm, :])
    u = _dot(h, wu_ref[...])
    o_ref[...] = (_silu(conv) * u).astype(o_ref.dtype)


def _ffn_gate_up(h, wg, wu, cw, seq, tm=1024, tf=256):
    m, d = h.shape
    f = wg.shape[1]
    assert m % tm == 0 and f % tf == 0 and seq % tm == 0 and tm % FFN_HALO == 0
    halo_blocks = tm // FFN_HALO
    return pl.pallas_call(
        functools.partial(_ffn_gate_up_kernel, seq // tm),
        grid=(m // tm, f // tf),
        in_specs=[pl.BlockSpec((tm, d), lambda i, j: (i, 0)),
                  pl.BlockSpec((FFN_HALO, d), lambda i, j: (jnp.maximum(i * halo_blocks - 1, 0), 0)),
                  pl.BlockSpec((d, tf), lambda i, j: (0, j)),
                  pl.BlockSpec((d, tf), lambda i, j: (0, j)),
                  pl.BlockSpec((FFN_CONV, tf), lambda i, j: (0, j))],
        out_specs=pl.BlockSpec((tm, tf), lambda i, j: (i, j)),
        out_shape=jax.ShapeDtypeStruct((m, f), bf16),
        scratch_shapes=[pltpu.VMEM((tm + FFN_HALO, tf), f32)],
        compiler_params=pltpu.CompilerParams(
            dimension_semantics=("parallel", "arbitrary"), vmem_limit_bytes=VMEM_LIMIT_BYTES),
        name="ffn_gate_up",
    )(h, h, wg, wu, cw)


def _ssd_kernel(xbc_ref, z_ref, nar_ref, cw_ref, cb_ref, dtb_ref, alog_ref, dsk_ref, nw_ref, exp_ref,
                o_ref, buf_ref, state_ref):
    q = z_ref.shape[0]
    hpg = SSD_HEADS // SSD_GROUPS
    gw = MIX_W // SSD_GROUPS

    @pl.when(pl.program_id(1) == 0)
    def _():
        buf_ref[0:SUBLANES, :] = jnp.zeros((SUBLANES, SSD_XBC), f32)
        state_ref[...] = jnp.zeros_like(state_ref)

    xc = _silu(_causal_conv(xbc_ref[...], buf_ref, cw_ref[...], SSD_CONV) + cb_ref[...])
    x = xc[:, :MIX_W]
    dt = _softplus(nar_ref[...] + dtb_ref[...])
    d_a = dt * (-jnp.exp(alog_ref[...]))
    a_cs = _sel_dot_l(_tri_ones(q), d_a)
    a_cs_t = a_cs.T
    a_end = a_cs[q - 1:q, :]
    expand = exp_ref[...]
    dt_e = _sel_dot_r(dt, expand)
    ea_e = _sel_dot_r(jnp.exp(a_cs), expand)
    de_e = _sel_dot_r(jnp.exp(a_end - a_cs), expand)
    xdt = x * dt_e
    xde = (xdt * de_e).astype(bf16)
    causal = _iota2((q, q), 0) >= _iota2((q, q), 1)
    lane = _iota2((1, LANES), 1)
    y_groups = []
    for g in range(SSD_GROUPS):
        bm = xc[:, MIX_W + g * SSD_STATE:MIX_W + (g + 1) * SSD_STATE].astype(bf16)
        cm = xc[:, MIX_W + (SSD_GROUPS + g) * SSD_STATE:MIX_W + (SSD_GROUPS + g + 1) * SSD_STATE].astype(bf16)
        scores = _dot_nt(cm, bm)
        pieces = []
        for pair in range(hpg // 2):
            c0 = g * gw + pair * LANES
            xp = xdt[:, c0:c0 + LANES]
            acc = None
            for half in range(2):
                h = g * hpg + pair * 2 + half
                seg = a_cs[:, h:h + 1] - a_cs_t[h:h + 1, :]
                p = (scores * jnp.exp(jnp.where(causal, seg, -jnp.inf))).astype(bf16)
                in_half = (lane >= half * SSD_HEAD_DIM) & (lane < (half + 1) * SSD_HEAD_DIM)
                term = _dot(p, jnp.where(in_half, xp, 0.0).astype(bf16))
                acc = term if acc is None else acc + term
            pieces.append(acc)
        y_intra = jnp.concatenate(pieces, axis=1)
        s_prev = state_ref[g]
        y_inter = _dot(cm, s_prev.astype(bf16)) * ea_e[:, g * gw:(g + 1) * gw]
        state_ref[g] = s_prev * ea_e[q - 1:q, g * gw:(g + 1) * gw] + _dot_tn(bm, xde[:, g * gw:(g + 1) * gw])
        y_groups.append(y_intra + y_inter)
    y = jnp.concatenate(y_groups, axis=1) + x * dsk_ref[...]
    y = y * _silu(z_ref[...])
    nw = nw_ref[...]
    out = [_group_rms(y[:, g * gw:(g + 1) * gw], nw[:, g * gw:(g + 1) * gw]) for g in range(SSD_GROUPS)]
    o_ref[...] = jnp.concatenate(out, axis=1).astype(o_ref.dtype)


def _ssd_mixer(proj, bsz, seq, conv_w, conv_b, dt_bias, a_log, d_skip, norm_w):
    q = SSD_Q
    assert seq % q == 0
    nc = seq // q
    head_of_col = jnp.arange(MIX_W) // SSD_HEAD_DIM
    expand = (jnp.arange(LANES)[:, None] == head_of_col[None, :]).astype(bf16)
    row = lambda i, c: (i * nc + c, 0)
    const = lambda i, c: (0, 0)
    return pl.pallas_call(
        _ssd_kernel,
        grid=(bsz, nc),
        in_specs=[pl.BlockSpec((q, SSD_XBC), lambda i, c: (i * nc + c, COL_SSD_XBC // SSD_XBC)),
                  pl.BlockSpec((q, MIX_W), lambda i, c: (i * nc + c, COL_SSD_Z // MIX_W)),
                  pl.BlockSpec((q, LANES), lambda i, c: (i * nc + c, COL_NARROW // LANES)),
                  pl.BlockSpec((SSD_CONV, SSD_XBC), const),
                  pl.BlockSpec((1, SSD_XBC), const),
                  pl.BlockSpec((1, LANES), const),
                  pl.BlockSpec((1, LANES), const),
                  pl.BlockSpec((1, MIX_W), const),
                  pl.BlockSpec((1, MIX_W), const),
                  pl.BlockSpec((LANES, MIX_W), const)],
        out_specs=pl.BlockSpec((q, MIX_W), row),
        out_shape=jax.ShapeDtypeStruct((bsz * seq, MIX_W), bf16),
        scratch_shapes=[pltpu.VMEM((q + SUBLANES, SSD_XBC), f32),
                        pltpu.VMEM((SSD_GROUPS, SSD_STATE, MIX_W // SSD_GROUPS), f32)],
        compiler_params=pltpu.CompilerParams(
            dimension_semantics=("parallel", "arbitrary"), vmem_limit_bytes=VMEM_LIMIT_BYTES),
        name="ssd_mixer",
    )(proj, proj, proj, conv_w.astype(f32), conv_b.reshape(1, SSD_XBC).astype(f32),
      _pad_lanes(dt_bias, LANE_SSD_DT), _pad_lanes(a_log, LANE_SSD_DT),
      jnp.repeat(d_skip.astype(f32), SSD_HEAD_DIM).reshape(1, MIX_W), norm_w.reshape(1, MIX_W).astype(f32), expand)


def _s5_tables(lam_re, lam_im, b_re, b_im, c_re, c_im, log_dt, scan_len):
    hp = lax.Precision.HIGHEST
    t = S5_T
    dt = jnp.exp(log_dt.astype(f32))[:, None]
    ar, ai = lam_re.astype(f32) * dt, lam_im.astype(f32) * dt
    mag = jnp.exp(ar)
    lb_re, lb_im = mag * jnp.cos(ai), mag * jnp.sin(ai)
    den = lam_re * lam_re + lam_im * lam_im
    f_re = ((lb_re - 1.0) * lam_re + lb_im * lam_im) / den
    f_im = (lb_im * lam_re - (lb_re - 1.0) * lam_im) / den
    bb_re = f_re[..., None] * b_re - f_im[..., None] * b_im
    bb_im = f_re[..., None] * b_im + f_im[..., None] * b_re
    d = jnp.arange(t + 1, dtype=f32)[:, None, None]
    pw_mag = jnp.exp(d * ar)
    pw_re, pw_im = pw_mag * jnp.cos(d * ai), pw_mag * jnp.sin(d * ai)
    cr, ci = c_re.astype(f32), c_im.astype(f32)
    m_re = cr[None] * pw_re[:, :, None, :] - ci[None] * pw_im[:, :, None, :]
    m_im = cr[None] * pw_im[:, :, None, :] + ci[None] * pw_re[:, :, None, :]
    kern = (jnp.einsum('dgop,gpi->dgoi', m_re[:t], bb_re, precision=hp)
            - jnp.einsum('dgop,gpi->dgoi', m_im[:t], bb_im, precision=hp))
    jj, tt = jnp.arange(t)[:, None], jnp.arange(t)[None, :]
    lag = jnp.clip(tt - jj, 0, t - 1)
    toep = jnp.where((tt >= jj)[:, :, None, None, None], kern[lag], 0.0)
    toep = toep.transpose(2, 0, 4, 1, 3).reshape(S5_GROUPS, t * S5_GROUP_CH, t * S5_GROUP_CH)
    rev_re, rev_im = pw_re[t - 1::-1][:t], pw_im[t - 1::-1][:t]
    v_re = rev_re[..., None] * bb_re[None] - rev_im[..., None] * bb_im[None]
    v_im = rev_re[..., None] * bb_im[None] + rev_im[..., None] * bb_re[None]
    vmat = jnp.concatenate([v_re.transpose(1, 0, 3, 2), v_im.transpose(1, 0, 3, 2)], axis=-1)
    vmat = vmat.reshape(S5_GROUPS, t * S5_GROUP_CH, 2 * S5_STATE)
    w_re = m_re[1:].transpose(1, 3, 0, 2).reshape(S5_GROUPS, S5_STATE, t * S5_GROUP_CH)
    w_im = m_im[1:].transpose(1, 3, 0, 2).reshape(S5_GROUPS, S5_STATE, t * S5_GROUP_CH)
    wmat = jnp.concatenate([w_re, -w_im], axis=1)
    a_re, a_im = pw_re[t], pw_im[t]
    c1, c2 = [], []
    for _ in range(max(1, (scan_len - 1).bit_length())):
        c1.append(jnp.concatenate([a_re, a_re], axis=-1))
        c2.append(jnp.concatenate([-a_im, a_im], axis=-1))
        a_re, a_im = a_re * a_re - a_im * a_im, 2.0 * a_re * a_im
    return toep.astype(bf16), vmat.astype(bf16), wmat.astype(bf16), jnp.stack(c1, axis=1), jnp.stack(c2, axis=1)


def _s5_scan_kernel(chunks_per_seq, u_ref, toep_ref, v_ref, w_ref, c1_ref, c2_ref, o_ref):
    u = u_ref[0].astype(bf16)
    rows = u.shape[0]
    z = _dot(u, v_ref[0])
    pos = _iota2((rows, 1), 0) & (chunks_per_seq - 1)
    c1, c2 = c1_ref[0], c2_ref[0]
    s = z
    for k in range(c1.shape[0]):
        step = 1 << k
        if step >= chunks_per_seq:
            break
        sh = jnp.where(pos >= step, pltpu.roll(s, step, 0), 0.0)
        s = s + sh * c1[k:k + 1, :] + pltpu.roll(sh, S5_STATE, 1) * c2[k:k + 1, :]
    s_in = jnp.where(pos >= 1, pltpu.roll(s, 1, 0), 0.0)
    o_ref[0] = _dot(u, toep_ref[0]) + _dot(s_in.astype(bf16), w_ref[0])


def _s5_glu_kernel(y_ref, u_ref, d_ref, gw_ref, gb_ref, o_ref):
    y = y_ref[...] + d_ref[...] * u_ref[...]
    y = 0.5 * y * (1.0 + jnp.tanh(math.sqrt(2.0 / math.pi) * (y + 0.044715 * (y * y * y))))
    gate = _dot(y.astype(bf16), gw_ref[...]) + gb_ref[...]
    o_ref[...] = (y * jax.nn.sigmoid(gate)).astype(o_ref.dtype)


def _s5_mixer(proj, bsz, seq, lam_re, lam_im, b_re, b_im, c_re, c_im, d_skip, log_dt, glu_w, glu_b, tm=512):
    t, gch = S5_T, S5_GROUP_CH
    assert seq % t == 0
    cps = seq // t
    assert cps & (cps - 1) == 0
    rows = bsz * cps
    m = bsz * seq
    toep, vmat, wmat, c1, c2 = _s5_tables(lam_re, lam_im, b_re, b_im, c_re, c_im, log_dt, cps)
    nk = c1.shape[1]
    u = proj[:, COL_S5_U:COL_S5_U + MIX_W]
    ug = u.reshape(rows, t, S5_GROUPS, gch).transpose(2, 0, 1, 3).reshape(S5_GROUPS, rows, t * gch)
    per_group = lambda g: (g, 0, 0)
    yg = pl.pallas_call(
        functools.partial(_s5_scan_kernel, cps),
        grid=(S5_GROUPS,),
        in_specs=[pl.BlockSpec((1, rows, t * gch), per_group),
                  pl.BlockSpec((1, t * gch, t * gch), per_group),
                  pl.BlockSpec((1, t * gch, 2 * S5_STATE), per_group),
                  pl.BlockSpec((1, 2 * S5_STATE, t * gch), per_group),
                  pl.BlockSpec((1, nk, 2 * S5_STATE), per_group),
                  pl.BlockSpec((1, nk, 2 * S5_STATE), per_group)],
        out_specs=pl.BlockSpec((1, rows, t * gch), per_group),
        out_shape=jax.ShapeDtypeStruct((S5_GROUPS, rows, t * gch), f32),
        compiler_params=pltpu.CompilerParams(dimension_semantics=("parallel",)),
        name="s5_scan",
    )(ug, toep, vmat, wmat, c1, c2)
    y = yg.reshape(S5_GROUPS, rows, t, gch).transpose(1, 2, 0, 3).reshape(m, MIX_W)
    assert m % tm == 0
    return pl.pallas_call(
        _s5_glu_kernel,
        grid=(m // tm,),
        in_specs=[pl.BlockSpec((tm, MIX_W), lambda i: (i, 0)),
                  pl.BlockSpec((tm, MIX_W), lambda i: (i, COL_S5_U // MIX_W)),
                  pl.BlockSpec((1, MIX_W), lambda i: (0, 0)),
                  pl.BlockSpec((MIX_W, MIX_W), lambda i: (0, 0)),
                  pl.BlockSpec((1, MIX_W), lambda i: (0, 0))],
        out_specs=pl.BlockSpec((tm, MIX_W), lambda i: (i, 0)),
        out_shape=jax.ShapeDtypeStruct((m, MIX_W), bf16),
        compiler_params=pltpu.CompilerParams(dimension_semantics=("parallel",)),
        name="s5_glu",
    )(y, proj, d_skip.reshape(1, MIX_W).astype(f32), glu_w.astype(bf16), glu_b.reshape(1, MIX_W).astype(f32))


def _gla_kernel(qk_ref, v_ref, r_ref, nar_ref, w2_ref, gb_ref, nw_ref, sel_ref, o_ref, state_ref):
    q = v_ref.shape[0]
    levels = q.bit_length() - 1
    dkw = GLA_HEADS * GLA_DK

    @pl.when(pl.program_id(1) == 0)
    def _():
        state_ref[...] = jnp.zeros_like(state_ref)

    gate = _dot(nar_ref[...].astype(bf16), w2_ref[...]) + gb_ref[...]
    log_a = -_softplus(-gate) * (1.0 / GLA_GATE_NORM)
    b = _sel_dot_l(_tri_ones(q), log_a)
    refs = _sel_dot_l(sel_ref[...], b)
    qk = qk_ref[...]
    qs = qk[:, :dkw] * (GLA_DK ** -0.5)
    ks = qk[:, dkw:]
    v = v_ref[...]
    rgate = _silu(r_ref[...])
    nw = nw_ref[...]
    row = _iota2((q, 1), 0)
    ri, ci = _iota2((q, q), 0), _iota2((q, q), 1)
    eb = jnp.exp(b)
    b_end = b[q - 1:q, :]
    k_end = (ks * jnp.exp(b_end - b)).astype(bf16)
    out = []
    for h in range(GLA_HEADS):
        sl = slice(h * GLA_DK, (h + 1) * GLA_DK)
        qh, kh, bh = qs[:, sl], ks[:, sl], b[:, sl]
        vh = v[:, h * GLA_DV:(h + 1) * GLA_DV].astype(bf16)
        scores = jnp.where(ri == ci, jnp.sum(qh * kh, axis=-1, keepdims=True), 0.0)
        for lv in range(levels):
            s = 1 << lv
            upper = ((row >> lv) & 1) == 1
            rl = refs[lv * q:(lv + 1) * q, sl]
            qt = (qh * jnp.exp(jnp.where(upper, bh - rl, -jnp.inf))).astype(bf16)
            kt = (kh * jnp.exp(jnp.where(upper, -jnp.inf, rl - bh))).astype(bf16)
            same_block = (ri >> (lv + 1)) == (ci >> (lv + 1))
            scores = scores + jnp.where(same_block, _dot_nt(qt, kt), 0.0)
        s_prev = state_ref[h]
        o = _dot(scores.astype(bf16), vh) + _dot_nt((qh * eb[:, sl]).astype(bf16), s_prev.astype(bf16))
        state_ref[h] = s_prev * jnp.exp(b_end[:, sl]) + _dot_tn(vh, k_end[:, sl])
        o = _group_rms(o, nw) * rgate[:, h * GLA_DV:(h + 1) * GLA_DV]
        out.append(o)
    o_ref[...] = jnp.concatenate(out, axis=1).astype(o_ref.dtype)


def _gla_mixer(proj, bsz, seq, gate_w2, gate_b, norm_w):
    q = GLA_Q
    assert seq % q == 0 and q & (q - 1) == 0
    nc = seq // q
    levels = q.bit_length() - 1
    i = jnp.arange(q)
    sel = jnp.concatenate(
        [(((i // (2 << lv)) * (2 << lv) + (1 << lv))[:, None] == i[None, :]) for lv in range(levels)], axis=0).astype(bf16)
    w2 = jnp.zeros((LANES, GLA_HEADS * GLA_DK), f32).at[LANE_GLA_G:LANE_GLA_G + GLA_GATE_RANK].set(gate_w2.astype(f32))
    const = lambda i, c: (0, 0)
    blk = lambda col: pl.BlockSpec((q, MIX_W), lambda i, c: (i * nc + c, col // MIX_W))
    return pl.pallas_call(
        _gla_kernel,
        grid=(bsz, nc),
        in_specs=[blk(COL_GLA_QK), blk(COL_GLA_V), blk(COL_GLA_R),
                  pl.BlockSpec((q, LANES), lambda i, c: (i * nc + c, COL_NARROW // LANES)),
                  pl.BlockSpec((LANES, GLA_HEADS * GLA_DK), const),
                  pl.BlockSpec((1, GLA_HEADS * GLA_DK), const),
                  pl.BlockSpec((1, GLA_DV), const),
                  pl.BlockSpec((levels * q, q), const)],
        out_specs=pl.BlockSpec((q, MIX_W), lambda i, c: (i * nc + c, 0)),
        out_shape=jax.ShapeDtypeStruct((bsz * seq, MIX_W), bf16),
        scratch_shapes=[pltpu.VMEM((GLA_HEADS, GLA_DV, GLA_DK), f32)],
        compiler_params=pltpu.CompilerParams(
            dimension_semantics=("parallel", "arbitrary"), vmem_limit_bytes=VMEM_LIMIT_BYTES),
        name="gla_mixer",
    )(proj, proj, proj, proj, w2.astype(bf16), gate_b.reshape(1, -1).astype(f32),
      norm_w.reshape(1, GLA_DV).astype(f32), sel)


def _gdn_kernel(qkv_ref, z_ref, nar_ref, cw_ref, alog_ref, dtb_ref, nw_ref, o_ref, buf_ref, state_ref):
    c = z_ref.shape[0]
    hw = GDN_HEADS * GDN_DK

    @pl.when(pl.program_id(1) == 0)
    def _():
        buf_ref[0:SUBLANES, :] = jnp.zeros((SUBLANES, 3 * MIX_W), f32)
        state_ref[...] = jnp.zeros_like(state_ref)

    qkv = _silu(_causal_conv(qkv_ref[...], buf_ref, cw_ref[...], GDN_CONV))
    nar = nar_ref[...]
    beta = jax.nn.sigmoid(nar)
    g = -jnp.exp(alog_ref[...]) * _softplus(nar + dtb_ref[...])
    gc = _sel_dot_l(_tri_ones(c), g)
    gc_t = gc.T
    eg = jnp.exp(gc)
    eg_rev = jnp.exp(gc[c - 1:c, :] - gc)
    ri, ci = _iota2((c, c), 0), _iota2((c, c), 1)
    z = z_ref[...]
    nw = nw_ref[...]
    out = []
    for h in range(GDN_HEADS):
        qh = qkv[:, h * GDN_DK:(h + 1) * GDN_DK]
        kh = qkv[:, hw + h * GDN_DK:hw + (h + 1) * GDN_DK]
        vh = qkv[:, 2 * hw + h * GDN_DV:2 * hw + (h + 1) * GDN_DV]
        qh = qh * lax.rsqrt(jnp.sum(qh * qh, axis=-1, keepdims=True) + 1e-6) * (GDN_DK ** -0.5)
        kh = kh * lax.rsqrt(jnp.sum(kh * kh, axis=-1, keepdims=True) + 1e-6)
        bh = beta[:, LANE_GDN_B + h:LANE_GDN_B + h + 1]
        la = LANE_GDN_A + h
        decay = jnp.exp(jnp.where(ri >= ci, gc[:, la:la + 1] - gc_t[la:la + 1, :], -jnp.inf))
        kb = kh * bh
        kh16 = kh.astype(bf16)
        a = jnp.where(ri > ci, _dot_nt(kb.astype(bf16), kh16) * decay, 0.0)
        p = -a
        t = jnp.where(ri == ci, 1.0, 0.0) + p
        for _ in range(c.bit_length() - 2):
            p = _dot3(p, p)
            t = t + _dot3(t, p)
        u = _dot3(t, vh * bh)
        w = _dot3(t, kb * eg[:, la:la + 1])
        s_prev = state_ref[h]
        s16 = s_prev.astype(bf16)
        v_new = u - _dot(w.astype(bf16), s16)
        attn = _dot_nt(qh.astype(bf16), kh16) * decay
        o = _dot((qh * eg[:, la:la + 1]).astype(bf16), s16) + _dot(attn.astype(bf16), v_new.astype(bf16))
        k_end = (kh * eg_rev[:, la:la + 1]).astype(bf16)
        state_ref[h] = s_prev * eg[c - 1:c, la:la + 1] + _dot_tn(k_end, v_new.astype(bf16))
        out.append(_group_rms(o, nw) * _silu(z[:, h * GDN_DV:(h + 1) * GDN_DV]))
    o_ref[...] = jnp.concatenate(out, axis=1).astype(o_ref.dtype)


def _gdn_mixer(proj, bsz, seq, conv_w, a_log, dt_bias, norm_w):
    c = GDN_C
    assert seq % c == 0 and c & (c - 1) == 0
    nc = seq // c
    const = lambda i, cc: (0, 0)
    return pl.pallas_call(
        _gdn_kernel,
        grid=(bsz, nc),
        in_specs=[pl.BlockSpec((c, 3 * MIX_W), lambda i, cc: (i * nc + cc, COL_GDN_QKV // (3 * MIX_W))),
                  pl.BlockSpec((c, MIX_W), lambda i, cc: (i * nc + cc, COL_GDN_Z // MIX_W)),
                  pl.BlockSpec((c, LANES), lambda i, cc: (i * nc + cc, COL_NARROW // LANES)),
                  pl.BlockSpec((GDN_CONV, 3 * MIX_W), const),
                  pl.BlockSpec((1, LANES), const),
                  pl.BlockSpec((1, LANES), const),
                  pl.BlockSpec((1, GDN_DV), const)],
        out_specs=pl.BlockSpec((c, MIX_W), lambda i, cc: (i * nc + cc, 0)),
        out_shape=jax.ShapeDtypeStruct((bsz * seq, MIX_W), bf16),
        scratch_shapes=[pltpu.VMEM((c + SUBLANES, 3 * MIX_W), f32),
                        pltpu.VMEM((GDN_HEADS, GDN_DK, GDN_DV), f32)],
        compiler_params=pltpu.CompilerParams(
            dimension_semantics=("parallel", "arbitrary"), vmem_limit_bytes=VMEM_LIMIT_BYTES),
        name="gdn_mixer",
    )(proj, proj, proj, conv_w.astype(f32), _pad_lanes(a_log, LANE_GDN_A), _pad_lanes(dt_bias, LANE_GDN_A),
      norm_w.reshape(1, GDN_DV).astype(f32))


def kernel(x, norm_mix_pre, norm_mix_post, norm_ffn_pre, norm_ffn_post, w_in, w_out, ssd_conv_w, ssd_conv_b, ssd_dt_bias, ssd_a_log, ssd_d, ssd_norm, s5_lambda_re, s5_lambda_im, s5_b_re, s5_b_im, s5_c_re, s5_c_im, s5_d, s5_log_dt, s5_glu_w, s5_glu_b, gla_gate_w2, gla_gate_b, gla_norm, gdn_conv_w, gdn_a_log, gdn_dt_bias, gdn_norm, ffn_w_gate, ffn_w_up, ffn_conv_w, ffn_w_down):
    bsz, seq, d = x.shape
    m = bsz * seq
    xf = x.reshape(m, d)
    for l in range(DEPTH):
        h = _prenorm(xf, norm_mix_pre[l])
        proj = _matmul(h, _permute_w_in(w_in[l]), f32, 1024, PROJ_TN)
        y_ssd = _ssd_mixer(proj, bsz, seq, ssd_conv_w[l], ssd_conv_b[l], ssd_dt_bias[l], ssd_a_log[l],
                           ssd_d[l], ssd_norm[l])
        y_s5 = _s5_mixer(proj, bsz, seq, s5_lambda_re[l], s5_lambda_im[l], s5_b_re[l], s5_b_im[l],
                         s5_c_re[l], s5_c_im[l], s5_d[l], s5_log_dt[l], s5_glu_w[l], s5_glu_b[l])
        y_gla = _gla_mixer(proj, bsz, seq, gla_gate_w2[l], gla_gate_b[l], gla_norm[l])
        y_gdn = _gdn_mixer(proj, bsz, seq, gdn_conv_w[l], gdn_a_log[l], gdn_dt_bias[l], gdn_norm[l])
        mixed = _outproj([y_ssd, y_s5, y_gla, y_gdn], w_out[l].astype(bf16))
        xf = _postnorm_residual(xf, mixed, norm_mix_post[l])

        h = _prenorm(xf, norm_ffn_pre[l])
        act = _ffn_gate_up(h, ffn_w_gate[l].astype(bf16), ffn_w_up[l].astype(bf16), ffn_conv_w[l], seq)
        down = _matmul_ksplit(act, ffn_w_down[l].astype(bf16), f32, 512, 1024, D_FF // 2)
        xf = _postnorm_residual(xf, down, norm_ffn_post[l])
    return xf.reshape(bsz, seq, d)
```

```python
import functools
import math

import jax
import jax.numpy as jnp
from jax import lax
from jax.experimental import pallas as pl
from jax.experimental.pallas import tpu as pltpu

D_MODEL = 4096
DEPTH = 2
MIX_W = D_MODEL // 4
RMS_EPS = 1e-6
D_FF = 11008
FFN_CONV = 3

SSD_HEAD_DIM = 64
SSD_HEADS = MIX_W // SSD_HEAD_DIM
SSD_STATE = 128
SSD_GROUPS = 2
SSD_CONV = 4
SSD_XBC = MIX_W + 2 * SSD_GROUPS * SSD_STATE
SSD_Q = 128

S5_GROUP_CH = 16
S5_GROUPS = MIX_W // S5_GROUP_CH
S5_STATE = 64
S5_T = 16
S5_GB = 8

GLA_HEADS = 4
GLA_DK = MIX_W // (2 * GLA_HEADS)
GLA_DV = MIX_W // GLA_HEADS
GLA_GATE_RANK = 16
GLA_GATE_NORM = 16.0
GLA_Q = 64

GDN_HEADS = 8
GDN_DK = MIX_W // GDN_HEADS
GDN_DV = MIX_W // GDN_HEADS
GDN_CONV = 4
GDN_C = 64

SPLIT_SIZES = (
    MIX_W, SSD_XBC, SSD_HEADS,
    MIX_W,
    GLA_HEADS * GLA_DK, GLA_HEADS * GLA_DK, MIX_W, MIX_W, GLA_GATE_RANK,
    GDN_HEADS * GDN_DK, GDN_HEADS * GDN_DK, GDN_HEADS * GDN_DV, MIX_W, GDN_HEADS, GDN_HEADS,
)
D_IN_PROJ = sum(SPLIT_SIZES)
(SEG_SSD_Z, SEG_SSD_XBC, SEG_SSD_DT, SEG_S5_U, SEG_GLA_Q, SEG_GLA_K, SEG_GLA_V, SEG_GLA_R, SEG_GLA_G,
 SEG_GDN_Q, SEG_GDN_K, SEG_GDN_V, SEG_GDN_Z, SEG_GDN_B, SEG_GDN_A) = range(len(SPLIT_SIZES))

LANES = 128
SUBLANES = 8
VMEM_LIMIT_BYTES = 56 * 1024 * 1024

N_PROJ_PAD = 10240
PROJ_TN = 1024
COL_GDN_QKV = 0
COL_SSD_XBC = 3072
COL_NARROW = 4608
COL_GLA_QK = 5120
COL_GLA_V = 6144
COL_GLA_R = 7168
COL_GDN_Z = 8192
COL_SSD_Z = 9216
LANE_SSD_DT = 0
LANE_GLA_G = 16
LANE_GDN_B = 32
LANE_GDN_A = 40
_DST_OFF = {
    SEG_GDN_Q: COL_GDN_QKV, SEG_GDN_K: COL_GDN_QKV + 1024, SEG_GDN_V: COL_GDN_QKV + 2048,
    SEG_SSD_XBC: COL_SSD_XBC,
    SEG_SSD_DT: COL_NARROW + LANE_SSD_DT, SEG_GLA_G: COL_NARROW + LANE_GLA_G,
    SEG_GDN_B: COL_NARROW + LANE_GDN_B, SEG_GDN_A: COL_NARROW + LANE_GDN_A,
    SEG_GLA_Q: COL_GLA_QK, SEG_GLA_K: COL_GLA_QK + 512, SEG_GLA_V: COL_GLA_V, SEG_GLA_R: COL_GLA_R,
    SEG_GDN_Z: COL_GDN_Z, SEG_SSD_Z: COL_SSD_Z,
}

f32 = jnp.float32
bf16 = jnp.bfloat16


def _permute_w_in(w):
    src = [0]
    for s in SPLIT_SIZES:
        src.append(src[-1] + s)
    order = sorted(_DST_OFF, key=_DST_OFF.get)
    parts, pos = [], 0
    for i in order:
        if _DST_OFF[i] > pos:
            parts.append(jnp.zeros((w.shape[0], _DST_OFF[i] - pos), w.dtype))
        parts.append(w[:, src[i]:src[i + 1]])
        pos = _DST_OFF[i] + SPLIT_SIZES[i]
    if N_PROJ_PAD > pos:
        parts.append(jnp.zeros((w.shape[0], N_PROJ_PAD - pos), w.dtype))
    return jnp.concatenate(parts, axis=1).astype(bf16)


def _s5_w_in(w):
    off = sum(SPLIT_SIZES[:SEG_S5_U])
    return w[:, off:off + MIX_W].astype(bf16)


def _pad_lanes(vec, offset):
    return jnp.zeros((1, LANES), f32).at[0, offset:offset + vec.shape[0]].set(vec.astype(f32))


def _dot(a, b):
    return jnp.dot(a, b, preferred_element_type=f32)


def _dot_nt(a, b):
    return lax.dot_general(a, b, (((1,), (1,)), ((), ())), preferred_element_type=f32)


def _dot_tn(a, b):
    return lax.dot_general(a, b, (((0,), (0,)), ((), ())), preferred_element_type=f32)


def _split3(x):
    hi = x.astype(bf16)
    r1 = x - hi.astype(f32)
    mid = r1.astype(bf16)
    lo = (r1 - mid.astype(f32)).astype(bf16)
    return hi, mid, lo


def _sel_dot_l(sel, x):
    hi, mid, lo = _split3(x)
    return _dot(sel, hi) + _dot(sel, mid) + _dot(sel, lo)


def _sel_dot_r(x, sel):
    hi, mid, lo = _split3(x)
    return _dot(hi, sel) + _dot(mid, sel) + _dot(lo, sel)


def _softplus(x):
    return jnp.maximum(x, 0.0) + jnp.log1p(jnp.exp(-jnp.abs(x)))


def _silu(x):
    return x * jax.nn.sigmoid(x)


def _iota2(shape, dim):
    return lax.broadcasted_iota(jnp.int32, shape, dim)


def _tri_ones(n):
    return (_iota2((n, n), 0) >= _iota2((n, n), 1)).astype(bf16)


def _causal_conv(x, buf_ref, cw, kw):
    q = x.shape[0]
    buf_ref[SUBLANES:SUBLANES + q, :] = x
    acc = cw[kw - 1:kw, :] * x
    for k in range(kw - 1):
        s = SUBLANES - (kw - 1) + k
        acc = acc + cw[k:k + 1, :] * buf_ref[s:s + q, :]
    buf_ref[0:SUBLANES, :] = x[q - SUBLANES:q, :]
    return acc


def _group_rms(y, nw):
    ms = jnp.mean(y * y, axis=-1, keepdims=True)
    return y * lax.rsqrt(ms + RMS_EPS) * nw


def _matmul_kernel(a_ref, w_ref, o_ref):
    o_ref[...] = _dot(a_ref[...], w_ref[...]).astype(o_ref.dtype)


def _matmul(a, w, out_dtype, tm, tn):
    m, k = a.shape
    n = w.shape[1]
    assert m % tm == 0 and n % tn == 0
    return pl.pallas_call(
        _matmul_kernel,
        grid=(m // tm, n // tn),
        in_specs=[pl.BlockSpec((tm, k), lambda i, j: (i, 0)),
                  pl.BlockSpec((k, tn), lambda i, j: (0, j))],
        out_specs=pl.BlockSpec((tm, tn), lambda i, j: (i, j)),
        out_shape=jax.ShapeDtypeStruct((m, n), out_dtype),
        compiler_params=pltpu.CompilerParams(
            dimension_semantics=("parallel", "arbitrary"), vmem_limit_bytes=VMEM_LIMIT_BYTES),
        name="matmul",
    )(a, w)


def _outproj_kernel(a0_ref, a1_ref, a2_ref, a3_ref, w_ref, o_ref):
    acc = _dot(a0_ref[...], w_ref[0:MIX_W, :])
    acc = acc + _dot(a1_ref[...], w_ref[MIX_W:2 * MIX_W, :])
    acc = acc + _dot(a2_ref[...], w_ref[2 * MIX_W:3 * MIX_W, :])
    acc = acc + _dot(a3_ref[...], w_ref[3 * MIX_W:4 * MIX_W, :])
    o_ref[...] = acc


def _outproj(parts, w, tm=1024, tn=1024):
    m = parts[0].shape[0]
    k, n = w.shape
    assert m % tm == 0 and n % tn == 0 and k == 4 * MIX_W
    a_spec = pl.BlockSpec((tm, MIX_W), lambda i, j: (i, 0))
    return pl.pallas_call(
        _outproj_kernel,
        grid=(m // tm, n // tn),
        in_specs=[a_spec, a_spec, a_spec, a_spec, pl.BlockSpec((k, tn), lambda i, j: (0, j))],
        out_specs=pl.BlockSpec((tm, tn), lambda i, j: (i, j)),
        out_shape=jax.ShapeDtypeStruct((m, n), f32),
        compiler_params=pltpu.CompilerParams(
            dimension_semantics=("parallel", "arbitrary"), vmem_limit_bytes=VMEM_LIMIT_BYTES),
        name="outproj",
    )(*parts, w)


def _matmul_ksplit_kernel(a_ref, w_ref, o_ref, acc_ref):
    k = pl.program_id(2)
    part = _dot(a_ref[...], w_ref[...])

    @pl.when(k == 0)
    def _():
        acc_ref[...] = part

    @pl.when(k != 0)
    def _():
        acc_ref[...] += part

    @pl.when(k == pl.num_programs(2) - 1)
    def _():
        o_ref[...] = acc_ref[...].astype(o_ref.dtype)


def _matmul_ksplit(a, w, out_dtype, tm, tn, tk):
    m, k = a.shape
    n = w.shape[1]
    assert m % tm == 0 and n % tn == 0 and k % tk == 0
    return pl.pallas_call(
        _matmul_ksplit_kernel,
        grid=(m // tm, n // tn, k // tk),
        in_specs=[pl.BlockSpec((tm, tk), lambda i, j, kk: (i, kk)),
                  pl.BlockSpec((tk, tn), lambda i, j, kk: (kk, j))],
        out_specs=pl.BlockSpec((tm, tn), lambda i, j, kk: (i, j)),
        out_shape=jax.ShapeDtypeStruct((m, n), out_dtype),
        scratch_shapes=[pltpu.VMEM((tm, tn), f32)],
        compiler_params=pltpu.CompilerParams(
            dimension_semantics=("parallel", "arbitrary", "arbitrary"), vmem_limit_bytes=VMEM_LIMIT_BYTES),
        name="matmul_ksplit",
    )(a, w)


def _prenorm_kernel(x_ref, w_ref, o_ref):
    x = x_ref[...]
    ms = jnp.mean(x * x, axis=-1, keepdims=True)
    o_ref[...] = (x * lax.rsqrt(ms + RMS_EPS) * w_ref[...]).astype(o_ref.dtype)


def _prenorm(x, w, tm=256):
    m, d = x.shape
    assert m % tm == 0
    return pl.pallas_call(
        _prenorm_kernel,
        grid=(m // tm,),
        in_specs=[pl.BlockSpec((tm, d), lambda i: (i, 0)),
                  pl.BlockSpec((1, d), lambda i: (0, 0))],
        out_specs=pl.BlockSpec((tm, d), lambda i: (i, 0)),
        out_shape=jax.ShapeDtypeStruct((m, d), bf16),
        compiler_params=pltpu.CompilerParams(dimension_semantics=("parallel",)),
        name="prenorm",
    )(x, w.reshape(1, d))


def _postnorm_residual_kernel(x_ref, y_ref, w_ref, o_ref):
    y = y_ref[...]
    ms = jnp.mean(y * y, axis=-1, keepdims=True)
    o_ref[...] = x_ref[...] + y * lax.rsqrt(ms + RMS_EPS) * w_ref[...]


def _postnorm_residual_prenorm_kernel(x_ref, y_ref, w_ref, w2_ref, o_ref, h_ref):
    y = y_ref[...]
    x = x_ref[...] + y * lax.rsqrt(jnp.mean(y * y, axis=-1, keepdims=True) + RMS_EPS) * w_ref[...]
    o_ref[...] = x
    h_ref[...] = (x * lax.rsqrt(jnp.mean(x * x, axis=-1, keepdims=True) + RMS_EPS) * w2_ref[...]).astype(h_ref.dtype)


def _postnorm_residual_prenorm(x, y, w_post, w_pre, tm=256):
    m, d = x.shape
    assert m % tm == 0
    row = pl.BlockSpec((tm, d), lambda i: (i, 0))
    vec = pl.BlockSpec((1, d), lambda i: (0, 0))
    return pl.pallas_call(
        _postnorm_residual_prenorm_kernel,
        grid=(m // tm,),
        in_specs=[row, row, vec, vec],
        out_specs=[row, row],
        out_shape=[jax.ShapeDtypeStruct((m, d), f32), jax.ShapeDtypeStruct((m, d), bf16)],
        compiler_params=pltpu.CompilerParams(dimension_semantics=("parallel",)),
        name="postnorm_residual_prenorm",
    )(x, y, w_post.reshape(1, d), w_pre.reshape(1, d))


def _postnorm_residual(x, y, w, tm=256):
    m, d = x.shape
    assert m % tm == 0
    return pl.pallas_call(
        _postnorm_residual_kernel,
        grid=(m // tm,),
        in_specs=[pl.BlockSpec((tm, d), lambda i: (i, 0)),
                  pl.BlockSpec((tm, d), lambda i: (i, 0)),
                  pl.BlockSpec((1, d), lambda i: (0, 0))],
        out_specs=pl.BlockSpec((tm, d), lambda i: (i, 0)),
        out_shape=jax.ShapeDtypeStruct((m, d), f32),
        compiler_params=pltpu.CompilerParams(dimension_semantics=("parallel",)),
        name="postnorm_residual",
    )(x, y, w.reshape(1, d))


def _ffn_gate_up_kernel(blocks_per_seq, h_ref, wg_ref, wu_ref, cw_ref, o_ref, g_ref, tail_ref):
    i = pl.program_id(0)
    j = pl.program_id(1)
    tm = h_ref.shape[0]
    h = h_ref[...]
    g = _dot(h, wg_ref[...].astype(bf16))
    @pl.when(i % blocks_per_seq == 0)
    def _():
        g_ref[0:SUBLANES, :] = jnp.zeros((SUBLANES, g_ref.shape[1]), f32)

    @pl.when(i % blocks_per_seq != 0)
    def _():
        g_ref[0:SUBLANES, :] = tail_ref[j]

    g_ref[SUBLANES:SUBLANES + tm, :] = g
    tail_ref[j] = g[tm - SUBLANES:tm, :]
    cw = cw_ref[...]
    conv = (cw[2:3, :] * g
            + cw[1:2, :] * g_ref[SUBLANES - 1:SUBLANES - 1 + tm, :]
            + cw[0:1, :] * g_ref[SUBLANES - 2:SUBLANES - 2 + tm, :])
    u = _dot(h, wu_ref[...].astype(bf16))
    o_ref[...] = (_silu(conv) * u).astype(o_ref.dtype)


def _ffn_gate_up(h, wg, wu, cw, layer, seq, tm=1024, tf=256):
    m, d = h.shape
    f = wg.shape[2]
    assert m % tm == 0 and f % tf == 0 and seq % tm == 0
    return pl.pallas_call(
        functools.partial(_ffn_gate_up_kernel, seq // tm),
        grid=(m // tm, f // tf),
        in_specs=[pl.BlockSpec((tm, d), lambda i, j: (i, 0)),
                  pl.BlockSpec((None, d, tf), lambda i, j: (layer, 0, j)),
                  pl.BlockSpec((None, d, tf), lambda i, j: (layer, 0, j)),
                  pl.BlockSpec((FFN_CONV, tf), lambda i, j: (0, j))],
        out_specs=pl.BlockSpec((tm, tf), lambda i, j: (i, j)),
        out_shape=jax.ShapeDtypeStruct((m, f), bf16),
        scratch_shapes=[pltpu.VMEM((tm + SUBLANES, tf), f32),
                        pltpu.VMEM((f // tf, SUBLANES, tf), f32)],
        compiler_params=pltpu.CompilerParams(
            dimension_semantics=("arbitrary", "arbitrary"), vmem_limit_bytes=VMEM_LIMIT_BYTES),
        name="ffn_gate_up",
    )(h, wg, wu, cw)


def _ssd_kernel(xbc_ref, z_ref, nar_ref, cw_ref, cb_ref, dtb_ref, alog_ref, dsk_ref, nw_ref, exp_ref,
                o_ref, buf_ref, state_ref):
    q = z_ref.shape[0]
    hpg = SSD_HEADS // SSD_GROUPS
    gw = MIX_W // SSD_GROUPS

    @pl.when(pl.program_id(1) == 0)
    def _():
        buf_ref[0:SUBLANES, :] = jnp.zeros((SUBLANES, SSD_XBC), f32)
        state_ref[...] = jnp.zeros_like(state_ref)

    xc = _silu(_causal_conv(xbc_ref[...], buf_ref, cw_ref[...], SSD_CONV) + cb_ref[...])
    x = xc[:, :MIX_W]
    dt = _softplus(nar_ref[...] + dtb_ref[...])
    d_a = dt * (-jnp.exp(alog_ref[...]))
    a_cs = _sel_dot_l(_tri_ones(q), d_a)
    a_cs_t = a_cs.T
    a_end = a_cs[q - 1:q, :]
    expand = exp_ref[...]
    dt_e = _sel_dot_r(dt, expand)
    ea_e = _sel_dot_r(jnp.exp(a_cs), expand)
    de_e = _sel_dot_r(jnp.exp(a_end - a_cs), expand)
    xdt = x * dt_e
    xde = (xdt * de_e).astype(bf16)
    causal = _iota2((q, q), 0) >= _iota2((q, q), 1)
    lane = _iota2((1, LANES), 1)
    y_groups = []
    for g in range(SSD_GROUPS):
        bm = xc[:, MIX_W + g * SSD_STATE:MIX_W + (g + 1) * SSD_STATE].astype(bf16)
        cm = xc[:, MIX_W + (SSD_GROUPS + g) * SSD_STATE:MIX_W + (SSD_GROUPS + g + 1) * SSD_STATE].astype(bf16)
        scores = _dot_nt(cm, bm)
        pieces = []
        for pair in range(hpg // 2):
            c0 = g * gw + pair * LANES
            xp = xdt[:, c0:c0 + LANES]
            acc = None
            for half in range(2):
                h = g * hpg + pair * 2 + half
                seg = a_cs[:, h:h + 1] - a_cs_t[h:h + 1, :]
                p = (scores * jnp.exp(jnp.where(causal, seg, -jnp.inf))).astype(bf16)
                in_half = (lane >= half * SSD_HEAD_DIM) & (lane < (half + 1) * SSD_HEAD_DIM)
                term = _dot(p, jnp.where(in_half, xp, 0.0).astype(bf16))
                acc = term if acc is None else acc + term
            pieces.append(acc)
        y_intra = jnp.concatenate(pieces, axis=1)
        s_prev = state_ref[g]
        y_inter = _dot(cm, s_prev.astype(bf16)) * ea_e[:, g * gw:(g + 1) * gw]
        state_ref[g] = s_prev * ea_e[q - 1:q, g * gw:(g + 1) * gw] + _dot_tn(bm, xde[:, g * gw:(g + 1) * gw])
        y_groups.append(y_intra + y_inter)
    y = jnp.concatenate(y_groups, axis=1) + x * dsk_ref[...]
    y = y * _silu(z_ref[...])
    nw = nw_ref[...]
    out = [_group_rms(y[:, g * gw:(g + 1) * gw], nw[:, g * gw:(g + 1) * gw]) for g in range(SSD_GROUPS)]
    o_ref[...] = jnp.concatenate(out, axis=1).astype(o_ref.dtype)


def _ssd_mixer(proj, bsz, seq, conv_w, conv_b, dt_bias, a_log, d_skip, norm_w):
    q = SSD_Q
    assert seq % q == 0
    nc = seq // q
    head_of_col = jnp.arange(MIX_W) // SSD_HEAD_DIM
    expand = (jnp.arange(LANES)[:, None] == head_of_col[None, :]).astype(bf16)
    row = lambda i, c: (i * nc + c, 0)
    const = lambda i, c: (0, 0)
    return pl.pallas_call(
        _ssd_kernel,
        grid=(bsz, nc),
        in_specs=[pl.BlockSpec((q, SSD_XBC), lambda i, c: (i * nc + c, COL_SSD_XBC // SSD_XBC)),
                  pl.BlockSpec((q, MIX_W), lambda i, c: (i * nc + c, COL_SSD_Z // MIX_W)),
                  pl.BlockSpec((q, LANES), lambda i, c: (i * nc + c, COL_NARROW // LANES)),
                  pl.BlockSpec((SSD_CONV, SSD_XBC), const),
                  pl.BlockSpec((1, SSD_XBC), const),
                  pl.BlockSpec((1, LANES), const),
                  pl.BlockSpec((1, LANES), const),
                  pl.BlockSpec((1, MIX_W), const),
                  pl.BlockSpec((1, MIX_W), const),
                  pl.BlockSpec((LANES, MIX_W), const)],
        out_specs=pl.BlockSpec((q, MIX_W), row),
        out_shape=jax.ShapeDtypeStruct((bsz * seq, MIX_W), bf16),
        scratch_shapes=[pltpu.VMEM((q + SUBLANES, SSD_XBC), f32),
                        pltpu.VMEM((SSD_GROUPS, SSD_STATE, MIX_W // SSD_GROUPS), f32)],
        compiler_params=pltpu.CompilerParams(
            dimension_semantics=("parallel", "arbitrary"), vmem_limit_bytes=VMEM_LIMIT_BYTES),
        name="ssd_mixer",
    )(proj, proj, proj, conv_w.astype(f32), conv_b.reshape(1, SSD_XBC).astype(f32),
      _pad_lanes(dt_bias, LANE_SSD_DT), _pad_lanes(a_log, LANE_SSD_DT),
      jnp.repeat(d_skip.astype(f32), SSD_HEAD_DIM).reshape(1, MIX_W), norm_w.reshape(1, MIX_W).astype(f32), expand)


def _s5_tables(lam_re, lam_im, b_re, b_im, c_re, c_im, log_dt, scan_len):
    hp = lax.Precision.HIGHEST
    t, gb, nb, nh, ns = S5_T, S5_GB, S5_GROUPS // S5_GB, S5_GROUP_CH, S5_STATE
    dt = jnp.exp(log_dt.astype(f32))[:, None]
    ar, ai = lam_re.astype(f32) * dt, lam_im.astype(f32) * dt
    mag = jnp.exp(ar)
    lb_re, lb_im = mag * jnp.cos(ai), mag * jnp.sin(ai)
    den = lam_re * lam_re + lam_im * lam_im
    f_re = ((lb_re - 1.0) * lam_re + lb_im * lam_im) / den
    f_im = (lb_im * lam_re - (lb_re - 1.0) * lam_im) / den
    bb_re = f_re[..., None] * b_re - f_im[..., None] * b_im
    bb_im = f_re[..., None] * b_im + f_im[..., None] * b_re
    d = jnp.arange(t + 1, dtype=f32)[:, None, None]
    pw_mag = jnp.exp(d * ar)
    pw_re, pw_im = pw_mag * jnp.cos(d * ai), pw_mag * jnp.sin(d * ai)
    cr, ci = c_re.astype(f32), c_im.astype(f32)
    m_re = cr[None] * pw_re[:, :, None, :] - ci[None] * pw_im[:, :, None, :]
    m_im = cr[None] * pw_im[:, :, None, :] + ci[None] * pw_re[:, :, None, :]
    kern = (jnp.einsum('dgop,gpi->dgoi', m_re[:t], bb_re, precision=hp)
            - jnp.einsum('dgop,gpi->dgoi', m_im[:t], bb_im, precision=hp))
    eye = jnp.eye(gb, dtype=f32)

    def block_diag(x):
        shape_eye = [1] * (x.ndim + 1)
        shape_eye[2] = gb
        shape_eye[-2] = gb
        return x[..., None, :] * eye.reshape(shape_eye)

    jj, tt = jnp.arange(t)[:, None], jnp.arange(t)[None, :]
    lag = jnp.clip(tt - jj, 0, t - 1)
    toep = jnp.where((tt >= jj)[:, :, None, None, None], kern[lag], 0.0)
    toep = toep.reshape(t, t, nb, gb, nh, nh).transpose(2, 0, 3, 5, 1, 4)
    toep = block_diag(toep).reshape(nb, t * gb * nh, t * gb * nh)
    d_rev = jnp.arange(t - 1, -1, -1).astype(f32)[:, None, None]
    rev_mag = jnp.exp(d_rev * ar)
    rev_re, rev_im = rev_mag * jnp.cos(d_rev * ai), rev_mag * jnp.sin(d_rev * ai)
    v_re = rev_re[..., None] * bb_re[None] - rev_im[..., None] * bb_im[None]
    v_im = rev_re[..., None] * bb_im[None] + rev_im[..., None] * bb_re[None]
    vmat = jnp.stack([v_re, v_im], axis=0).reshape(2, t, nb, gb, ns, nh)
    vmat = vmat.transpose(2, 1, 3, 5, 0, 4)
    vmat = block_diag(vmat).reshape(nb, t * gb * nh, 2 * gb * ns)
    wmat = jnp.stack([m_re[1:], -m_im[1:]], axis=0).reshape(2, t, nb, gb, nh, ns)
    wmat = wmat.transpose(2, 0, 3, 5, 1, 4)
    wmat = block_diag(wmat).reshape(nb, 2 * gb * ns, t * gb * nh)
    a_re, a_im = pw_re[t].reshape(nb, gb * ns), pw_im[t].reshape(nb, gb * ns)
    c1, c2 = [], []
    for _ in range(max(1, (scan_len - 1).bit_length())):
        c1.append(jnp.concatenate([a_re, a_re], axis=-1))
        c2.append(jnp.concatenate([-a_im, a_im], axis=-1))
        a_re, a_im = a_re * a_re - a_im * a_im, 2.0 * a_re * a_im
    return toep.astype(bf16), vmat.astype(bf16), wmat.astype(bf16), jnp.stack(c1, axis=1), jnp.stack(c2, axis=1)


def _s5_scan_kernel(u_ref, toep_ref, v_ref, w_ref, c1_ref, c2_ref, o_ref):
    t, cps = u_ref.shape[0], u_ref.shape[1]
    half = c1_ref.shape[1] // 2
    x_all = jnp.concatenate([u_ref[j].astype(bf16) for j in range(t)], axis=1)
    z = _dot(x_all, v_ref[...])
    pos = _iota2((cps, 1), 0)
    c1, c2 = c1_ref[...], c2_ref[...]
    s = z
    for k in range(c1.shape[0]):
        step = 1 << k
        if step >= cps:
            break
        sh = jnp.where(pos >= step, pltpu.roll(s, step, 0), 0.0)
        s = s + sh * c1[k:k + 1, :] + pltpu.roll(sh, half, 1) * c2[k:k + 1, :]
    s_in = jnp.where(pos >= 1, pltpu.roll(s, 1, 0), 0.0)
    y = _dot(x_all, toep_ref[...]) + _dot(s_in.astype(bf16), w_ref[...])
    for j in range(t):
        o_ref[j] = y[:, j * LANES:(j + 1) * LANES]


def _s5_glu_kernel(y_ref, u_ref, d_ref, gw_ref, gb_ref, o_ref):
    y = y_ref[...] + d_ref[...] * u_ref[...]
    y = 0.5 * y * (1.0 + jnp.tanh(math.sqrt(2.0 / math.pi) * (y + 0.044715 * (y * y * y))))
    gate = _dot(y.astype(bf16), gw_ref[...]) + gb_ref[...]
    o_ref[...] = (y * jax.nn.sigmoid(gate)).astype(o_ref.dtype)


def _chunk_major(a, t):
    rows, w = a.shape
    return a.reshape(rows // t, t, w).transpose(1, 0, 2)


def _token_major(a3):
    t, chunks, w = a3.shape
    return a3.transpose(1, 0, 2).reshape(chunks * t, w)


def _s5_mixer(u3, bsz, seq, lam_re, lam_im, b_re, b_im, c_re, c_im, d_skip, log_dt, glu_w, glu_b, tm=512):
    t = S5_T
    assert seq % t == 0 and u3.shape == (t, bsz * seq // t, MIX_W)
    m = bsz * seq
    cps = seq // t
    nb = S5_GROUPS // S5_GB
    toep, vmat, wmat, c1, c2 = _s5_tables(lam_re, lam_im, b_re, b_im, c_re, c_im, log_dt, cps)
    nk = c1.shape[1]
    tw, sw = t * LANES, 2 * S5_GB * S5_STATE
    per_block = lambda blk, b: (blk, 0, 0)
    seq_block = pl.BlockSpec((t, cps, LANES), lambda blk, b: (0, b, blk))
    y3 = pl.pallas_call(
        _s5_scan_kernel,
        grid=(nb, bsz),
        in_specs=[seq_block,
                  pl.BlockSpec((None, tw, tw), per_block),
                  pl.BlockSpec((None, tw, sw), per_block),
                  pl.BlockSpec((None, sw, tw), per_block),
                  pl.BlockSpec((None, nk, sw), per_block),
                  pl.BlockSpec((None, nk, sw), per_block)],
        out_specs=seq_block,
        out_shape=jax.ShapeDtypeStruct(u3.shape, f32),
        compiler_params=pltpu.CompilerParams(
            dimension_semantics=("parallel", "parallel"), vmem_limit_bytes=VMEM_LIMIT_BYTES),
        name="s5_scan",
    )(u3, toep, vmat, wmat, c1, c2)
    assert m % tm == 0
    row_block = pl.BlockSpec((tm, MIX_W), lambda i: (i, 0))
    const = lambda i: (0, 0)
    out = pl.pallas_call(
        _s5_glu_kernel,
        grid=(m // tm,),
        in_specs=[row_block, row_block,
                  pl.BlockSpec((1, MIX_W), const),
                  pl.BlockSpec((MIX_W, MIX_W), const),
                  pl.BlockSpec((1, MIX_W), const)],
        out_specs=row_block,
        out_shape=jax.ShapeDtypeStruct((m, MIX_W), bf16),
        compiler_params=pltpu.CompilerParams(dimension_semantics=("parallel",)),
        name="s5_glu",
    )(y3.reshape(m, MIX_W), u3.reshape(m, MIX_W), d_skip.reshape(1, MIX_W).astype(f32), glu_w.astype(bf16),
      glu_b.reshape(1, MIX_W).astype(f32))
    return _token_major(out.reshape(t, m // t, MIX_W))


def _gla_kernel(qk_ref, v_ref, r_ref, nar_ref, w2_ref, gb_ref, nw_ref, sel_ref, o_ref, state_ref):
    q = v_ref.shape[0]
    levels = q.bit_length() - 1
    dkw = GLA_HEADS * GLA_DK

    @pl.when(pl.program_id(1) == 0)
    def _():
        state_ref[...] = jnp.zeros_like(state_ref)

    gate = _dot(nar_ref[...].astype(bf16), w2_ref[...]) + gb_ref[...]
    log_a = -_softplus(-gate) * (1.0 / GLA_GATE_NORM)
    b = _sel_dot_l(_tri_ones(q), log_a)
    refs = _sel_dot_l(sel_ref[...], b)
    qk = qk_ref[...]
    qs = qk[:, :dkw] * (GLA_DK ** -0.5)
    ks = qk[:, dkw:]
    v = v_ref[...]
    rgate = _silu(r_ref[...])
    nw = nw_ref[...]
    row = _iota2((q, 1), 0)
    ri, ci = _iota2((q, q), 0), _iota2((q, q), 1)
    eb = jnp.exp(b)
    b_end = b[q - 1:q, :]
    k_end = (ks * jnp.exp(b_end - b)).astype(bf16)
    out = []
    for h in range(GLA_HEADS):
        sl = slice(h * GLA_DK, (h + 1) * GLA_DK)
        qh, kh, bh = qs[:, sl], ks[:, sl], b[:, sl]
        vh = v[:, h * GLA_DV:(h + 1) * GLA_DV].astype(bf16)
        scores = jnp.where(ri == ci, jnp.sum(qh * kh, axis=-1, keepdims=True), 0.0)
        for lv in range(levels):
            s = 1 << lv
            upper = ((row >> lv) & 1) == 1
            rl = refs[lv * q:(lv + 1) * q, sl]
            qt = (qh * jnp.exp(jnp.where(upper, bh - rl, -jnp.inf))).astype(bf16)
            kt = (kh * jnp.exp(jnp.where(upper, -jnp.inf, rl - bh))).astype(bf16)
            same_block = (ri >> (lv + 1)) == (ci >> (lv + 1))
            scores = scores + jnp.where(same_block, _dot_nt(qt, kt), 0.0)
        s_prev = state_ref[h]
        o = _dot(scores.astype(bf16), vh) + _dot_nt((qh * eb[:, sl]).astype(bf16), s_prev.astype(bf16))
        state_ref[h] = s_prev * jnp.exp(b_end[:, sl]) + _dot_tn(vh, k_end[:, sl])
        o = _group_rms(o, nw) * rgate[:, h * GLA_DV:(h + 1) * GLA_DV]
        out.append(o)
    o_ref[...] = jnp.concatenate(out, axis=1).astype(o_ref.dtype)


def _gla_mixer(proj, bsz, seq, gate_w2, gate_b, norm_w):
    q = GLA_Q
    assert seq % q == 0 and q & (q - 1) == 0
    nc = seq // q
    levels = q.bit_length() - 1
    i = jnp.arange(q)
    sel = jnp.concatenate(
        [(((i // (2 << lv)) * (2 << lv) + (1 << lv))[:, None] == i[None, :]) for lv in range(levels)], axis=0).astype(bf16)
    w2 = jnp.zeros((LANES, GLA_HEADS * GLA_DK), f32).at[LANE_GLA_G:LANE_GLA_G + GLA_GATE_RANK].set(gate_w2.astype(f32))
    const = lambda i, c: (0, 0)
    blk = lambda col: pl.BlockSpec((q, MIX_W), lambda i, c: (i * nc + c, col // MIX_W))
    return pl.pallas_call(
        _gla_kernel,
        grid=(bsz, nc),
        in_specs=[blk(COL_GLA_QK), blk(COL_GLA_V), blk(COL_GLA_R),
                  pl.BlockSpec((q, LANES), lambda i, c: (i * nc + c, COL_NARROW // LANES)),
                  pl.BlockSpec((LANES, GLA_HEADS * GLA_DK), const),
                  pl.BlockSpec((1, GLA_HEADS * GLA_DK), const),
                  pl.BlockSpec((1, GLA_DV), const),
                  pl.BlockSpec((levels * q, q), const)],
        out_specs=pl.BlockSpec((q, MIX_W), lambda i, c: (i * nc + c, 0)),
        out_shape=jax.ShapeDtypeStruct((bsz * seq, MIX_W), bf16),
        scratch_shapes=[pltpu.VMEM((GLA_HEADS, GLA_DV, GLA_DK), f32)],
        compiler_params=pltpu.CompilerParams(
            dimension_semantics=("parallel", "arbitrary"), vmem_limit_bytes=VMEM_LIMIT_BYTES),
        name="gla_mixer",
    )(proj, proj, proj, proj, w2.astype(bf16), gate_b.reshape(1, -1).astype(f32),
      norm_w.reshape(1, GLA_DV).astype(f32), sel)


def _gdn_kernel(qkv_ref, z_ref, nar_ref, cw_ref, alog_ref, dtb_ref, nw_ref, o_ref, buf_ref, state_ref):
    c = z_ref.shape[0]
    hw = GDN_HEADS * GDN_DK

    @pl.when(pl.program_id(1) == 0)
    def _():
        buf_ref[0:SUBLANES, :] = jnp.zeros((SUBLANES, 3 * MIX_W), f32)
        state_ref[...] = jnp.zeros_like(state_ref)

    qkv = _silu(_causal_conv(qkv_ref[...], buf_ref, cw_ref[...], GDN_CONV))
    nar = nar_ref[...]
    beta = jax.nn.sigmoid(nar)
    g = -jnp.exp(alog_ref[...]) * _softplus(nar + dtb_ref[...])
    gc = _sel_dot_l(_tri_ones(c), g)
    gc_t = gc.T
    eg = jnp.exp(gc)
    eg_rev = jnp.exp(gc[c - 1:c, :] - gc)
    ri, ci = _iota2((c, c), 0), _iota2((c, c), 1)
    z = z_ref[...]
    nw = nw_ref[...]
    heads = range(GDN_HEADS)
    qn, kn, kb16, kh16, decay, egc, x_cur = [], [], [], [], [], [], []
    for h in heads:
        qh = qkv[:, h * GDN_DK:(h + 1) * GDN_DK]
        kh = qkv[:, hw + h * GDN_DK:hw + (h + 1) * GDN_DK]
        vh = qkv[:, 2 * hw + h * GDN_DV:2 * hw + (h + 1) * GDN_DV]
        qn.append(qh * lax.rsqrt(jnp.sum(qh * qh, axis=-1, keepdims=True) + 1e-6) * (GDN_DK ** -0.5))
        kh = kh * lax.rsqrt(jnp.sum(kh * kh, axis=-1, keepdims=True) + 1e-6)
        kn.append(kh)
        bh = beta[:, LANE_GDN_B + h:LANE_GDN_B + h + 1]
        la = LANE_GDN_A + h
        egc.append(eg[:, la:la + 1])
        decay.append(jnp.exp(jnp.where(ri >= ci, gc[:, la:la + 1] - gc_t[la:la + 1, :], -jnp.inf)))
        kb = kh * bh
        kb16.append(kb.astype(bf16))
        kh16.append(kh.astype(bf16))
        x_cur.append(jnp.concatenate([vh * bh, kb * egc[h]], axis=1))
    p_cur = [-jnp.where(ri > ci, _dot_nt(kb16[h], kh16[h]) * decay[h], 0.0) for h in heads]
    n_fac = c.bit_length() - 1
    for k in range(n_fac):
        p_hi = [p.astype(bf16) for p in p_cur]
        p_lo = [(p - ph.astype(f32)).astype(bf16) for p, ph in zip(p_cur, p_hi)]
        x_hi = [x.astype(bf16) for x in x_cur]
        x_lo = [(x - xh.astype(f32)).astype(bf16) for x, xh in zip(x_cur, x_hi)]
        px = [_dot(p_hi[h], x_hi[h]) + _dot(p_hi[h], x_lo[h]) + _dot(p_lo[h], x_hi[h]) for h in heads]
        x_cur = [x_cur[h] + px[h] for h in heads]
        if k + 1 < n_fac:
            p_cur = [_dot(p_hi[h], p_hi[h]) + _dot(p_hi[h], p_lo[h]) + _dot(p_lo[h], p_hi[h]) for h in heads]
    s_prev = [state_ref[h] for h in heads]
    s16 = [s.astype(bf16) for s in s_prev]
    v_new = [(x_cur[h][:, :GDN_DV] - _dot(x_cur[h][:, GDN_DV:].astype(bf16), s16[h])).astype(bf16) for h in heads]
    attn = [(_dot_nt(qn[h].astype(bf16), kh16[h]) * decay[h]).astype(bf16) for h in heads]
    o = [_dot((qn[h] * egc[h]).astype(bf16), s16[h]) + _dot(attn[h], v_new[h]) for h in heads]
    for h in heads:
        la = LANE_GDN_A + h
        k_end = (kn[h] * eg_rev[:, la:la + 1]).astype(bf16)
        state_ref[h] = s_prev[h] * eg[c - 1:c, la:la + 1] + _dot_tn(k_end, v_new[h])
    out = [_group_rms(o[h], nw) * _silu(z[:, h * GDN_DV:(h + 1) * GDN_DV]) for h in heads]
    o_ref[...] = jnp.concatenate(out, axis=1).astype(o_ref.dtype)


def _gdn_mixer(proj, bsz, seq, conv_w, a_log, dt_bias, norm_w):
    c = GDN_C
    assert seq % c == 0 and c & (c - 1) == 0
    nc = seq // c
    const = lambda i, cc: (0, 0)
    return pl.pallas_call(
        _gdn_kernel,
        grid=(bsz, nc),
        in_specs=[pl.BlockSpec((c, 3 * MIX_W), lambda i, cc: (i * nc + cc, COL_GDN_QKV // (3 * MIX_W))),
                  pl.BlockSpec((c, MIX_W), lambda i, cc: (i * nc + cc, COL_GDN_Z // MIX_W)),
                  pl.BlockSpec((c, LANES), lambda i, cc: (i * nc + cc, COL_NARROW // LANES)),
                  pl.BlockSpec((GDN_CONV, 3 * MIX_W), const),
                  pl.BlockSpec((1, LANES), const),
                  pl.BlockSpec((1, LANES), const),
                  pl.BlockSpec((1, GDN_DV), const)],
        out_specs=pl.BlockSpec((c, MIX_W), lambda i, cc: (i * nc + cc, 0)),
        out_shape=jax.ShapeDtypeStruct((bsz * seq, MIX_W), bf16),
        scratch_shapes=[pltpu.VMEM((c + SUBLANES, 3 * MIX_W), f32),
                        pltpu.VMEM((GDN_HEADS, GDN_DK, GDN_DV), f32)],
        compiler_params=pltpu.CompilerParams(
            dimension_semantics=("parallel", "arbitrary"), vmem_limit_bytes=VMEM_LIMIT_BYTES),
        name="gdn_mixer",
    )(proj, proj, proj, conv_w.astype(f32), _pad_lanes(a_log, LANE_GDN_A), _pad_lanes(dt_bias, LANE_GDN_A),
      norm_w.reshape(1, GDN_DV).astype(f32))


def kernel(x, norm_mix_pre, norm_mix_post, norm_ffn_pre, norm_ffn_post, w_in, w_out, ssd_conv_w, ssd_conv_b, ssd_dt_bias, ssd_a_log, ssd_d, ssd_norm, s5_lambda_re, s5_lambda_im, s5_b_re, s5_b_im, s5_c_re, s5_c_im, s5_d, s5_log_dt, s5_glu_w, s5_glu_b, gla_gate_w2, gla_gate_b, gla_norm, gdn_conv_w, gdn_a_log, gdn_dt_bias, gdn_norm, ffn_w_gate, ffn_w_up, ffn_conv_w, ffn_w_down):
    bsz, seq, d = x.shape
    m = bsz * seq
    xf = x.reshape(m, d)
    h = _prenorm(xf, norm_mix_pre[0])
    for l in range(DEPTH):
        proj = _matmul(h, _permute_w_in(w_in[l]), f32, 1024, PROJ_TN)
        y_ssd = _ssd_mixer(proj, bsz, seq, ssd_conv_w[l], ssd_conv_b[l], ssd_dt_bias[l], ssd_a_log[l],
                           ssd_d[l], ssd_norm[l])
        h_cm = _chunk_major(h, S5_T).reshape(m, d)
        u3 = _matmul(h_cm, _s5_w_in(w_in[l]), f32, 1024, 1024).reshape(S5_T, m // S5_T, MIX_W)
        y_s5 = _s5_mixer(u3, bsz, seq, s5_lambda_re[l], s5_lambda_im[l], s5_b_re[l], s5_b_im[l],
                         s5_c_re[l], s5_c_im[l], s5_d[l], s5_log_dt[l], s5_glu_w[l], s5_glu_b[l])
        y_gla = _gla_mixer(proj, bsz, seq, gla_gate_w2[l], gla_gate_b[l], gla_norm[l])
        y_gdn = _gdn_mixer(proj, bsz, seq, gdn_conv_w[l], gdn_a_log[l], gdn_dt_bias[l], gdn_norm[l])
        mixed = _outproj([y_ssd, y_s5, y_gla, y_gdn], w_out[l].astype(bf16))
        xf, h = _postnorm_residual_prenorm(xf, mixed, norm_mix_post[l], norm_ffn_pre[l])
        act = _ffn_gate_up(h, ffn_w_gate, ffn_w_up, ffn_conv_w[l], l, seq)
        down = _matmul_ksplit(act, ffn_w_down[l].astype(bf16), f32, 512, 1024, D_FF // 2)
        if l + 1 < DEPTH:
            xf, h = _postnorm_residual_prenorm(xf, down, norm_ffn_post[l], norm_mix_pre[l + 1])
        else:
            xf = _postnorm_residual(xf, down, norm_ffn_post[l])
    return xf.reshape(bsz, seq, d)
```

```python
import functools
import math

import jax
import jax.numpy as jnp
from jax import lax
from jax.experimental import pallas as pl
from jax.experimental.pallas import tpu as pltpu

D_MODEL = 4096
DEPTH = 2
MIX_W = D_MODEL // 4
RMS_EPS = 1e-6
D_FF = 11008
FFN_CONV = 3

SSD_HEAD_DIM = 64
SSD_HEADS = MIX_W // SSD_HEAD_DIM
SSD_STATE = 128
SSD_GROUPS = 2
SSD_CONV = 4
SSD_XBC = MIX_W + 2 * SSD_GROUPS * SSD_STATE
SSD_Q = 128

S5_GROUP_CH = 16
S5_GROUPS = MIX_W // S5_GROUP_CH
S5_STATE = 64
S5_T = 16
S5_GB = 8

GLA_HEADS = 4
GLA_DK = MIX_W // (2 * GLA_HEADS)
GLA_DV = MIX_W // GLA_HEADS
GLA_GATE_RANK = 16
GLA_GATE_NORM = 16.0
GLA_Q = 64

GDN_HEADS = 8
GDN_DK = MIX_W // GDN_HEADS
GDN_DV = MIX_W // GDN_HEADS
GDN_CONV = 4
GDN_C = 64

SPLIT_SIZES = (
    MIX_W, SSD_XBC, SSD_HEADS,
    MIX_W,
    GLA_HEADS * GLA_DK, GLA_HEADS * GLA_DK, MIX_W, MIX_W, GLA_GATE_RANK,
    GDN_HEADS * GDN_DK, GDN_HEADS * GDN_DK, GDN_HEADS * GDN_DV, MIX_W, GDN_HEADS, GDN_HEADS,
)
D_IN_PROJ = sum(SPLIT_SIZES)
(SEG_SSD_Z, SEG_SSD_XBC, SEG_SSD_DT, SEG_S5_U, SEG_GLA_Q, SEG_GLA_K, SEG_GLA_V, SEG_GLA_R, SEG_GLA_G,
 SEG_GDN_Q, SEG_GDN_K, SEG_GDN_V, SEG_GDN_Z, SEG_GDN_B, SEG_GDN_A) = range(len(SPLIT_SIZES))

LANES = 128
SUBLANES = 8
VMEM_LIMIT_BYTES = 56 * 1024 * 1024

N_PROJ_PAD = 10240
PROJ_TN = 1024
COL_GDN_QKV = 0
COL_SSD_XBC = 3072
COL_NARROW = 4608
COL_GLA_QK = 5120
COL_GLA_V = 6144
COL_GLA_R = 7168
COL_GDN_Z = 8192
COL_SSD_Z = 9216
LANE_SSD_DT = 0
LANE_GLA_G = 16
LANE_GDN_B = 32
LANE_GDN_A = 40
_DST_OFF = {
    SEG_GDN_Q: COL_GDN_QKV, SEG_GDN_K: COL_GDN_QKV + 1024, SEG_GDN_V: COL_GDN_QKV + 2048,
    SEG_SSD_XBC: COL_SSD_XBC,
    SEG_SSD_DT: COL_NARROW + LANE_SSD_DT, SEG_GLA_G: COL_NARROW + LANE_GLA_G,
    SEG_GDN_B: COL_NARROW + LANE_GDN_B, SEG_GDN_A: COL_NARROW + LANE_GDN_A,
    SEG_GLA_Q: COL_GLA_QK, SEG_GLA_K: COL_GLA_QK + 512, SEG_GLA_V: COL_GLA_V, SEG_GLA_R: COL_GLA_R,
    SEG_GDN_Z: COL_GDN_Z, SEG_SSD_Z: COL_SSD_Z,
}

f32 = jnp.float32
bf16 = jnp.bfloat16


def _permute_w_in(w):
    src = [0]
    for s in SPLIT_SIZES:
        src.append(src[-1] + s)
    order = sorted(_DST_OFF, key=_DST_OFF.get)
    parts, pos = [], 0
    for i in order:
        if _DST_OFF[i] > pos:
            parts.append(jnp.zeros((w.shape[0], _DST_OFF[i] - pos), w.dtype))
        parts.append(w[:, src[i]:src[i + 1]])
        pos = _DST_OFF[i] + SPLIT_SIZES[i]
    if N_PROJ_PAD > pos:
        parts.append(jnp.zeros((w.shape[0], N_PROJ_PAD - pos), w.dtype))
    return jnp.concatenate(parts, axis=1).astype(bf16)


def _s5_w_in(w):
    off = sum(SPLIT_SIZES[:SEG_S5_U])
    return w[:, off:off + MIX_W].astype(bf16)


def _pad_lanes(vec, offset):
    return jnp.zeros((1, LANES), f32).at[0, offset:offset + vec.shape[0]].set(vec.astype(f32))


def _dot(a, b):
    return jnp.dot(a, b, preferred_element_type=f32)


def _dot_nt(a, b):
    return lax.dot_general(a, b, (((1,), (1,)), ((), ())), preferred_element_type=f32)


def _dot_tn(a, b):
    return lax.dot_general(a, b, (((0,), (0,)), ((), ())), preferred_element_type=f32)


def _split3(x):
    hi = x.astype(bf16)
    r1 = x - hi.astype(f32)
    mid = r1.astype(bf16)
    lo = (r1 - mid.astype(f32)).astype(bf16)
    return hi, mid, lo


def _sel_dot_l(sel, x):
    hi, mid, lo = _split3(x)
    return _dot(sel, hi) + _dot(sel, mid) + _dot(sel, lo)


def _sel_dot_r(x, sel):
    hi, mid, lo = _split3(x)
    return _dot(hi, sel) + _dot(mid, sel) + _dot(lo, sel)


def _softplus(x):
    return jnp.maximum(x, 0.0) + jnp.log1p(jnp.exp(-jnp.abs(x)))


def _silu(x):
    return x * jax.nn.sigmoid(x)


def _iota2(shape, dim):
    return lax.broadcasted_iota(jnp.int32, shape, dim)


def _tri_ones(n):
    return (_iota2((n, n), 0) >= _iota2((n, n), 1)).astype(bf16)


def _causal_conv(x, buf_ref, cw, kw):
    q = x.shape[0]
    buf_ref[SUBLANES:SUBLANES + q, :] = x
    acc = cw[kw - 1:kw, :] * x
    for k in range(kw - 1):
        s = SUBLANES - (kw - 1) + k
        acc = acc + cw[k:k + 1, :] * buf_ref[s:s + q, :]
    buf_ref[0:SUBLANES, :] = x[q - SUBLANES:q, :]
    return acc


def _group_rms(y, nw):
    ms = jnp.mean(y * y, axis=-1, keepdims=True)
    return y * lax.rsqrt(ms + RMS_EPS) * nw


def _matmul_kernel(a_ref, w_ref, o_ref):
    o_ref[...] = _dot(a_ref[...], w_ref[...]).astype(o_ref.dtype)


def _matmul(a, w, out_dtype, tm, tn):
    m, k = a.shape
    n = w.shape[1]
    assert m % tm == 0 and n % tn == 0
    return pl.pallas_call(
        _matmul_kernel,
        grid=(m // tm, n // tn),
        in_specs=[pl.BlockSpec((tm, k), lambda i, j: (i, 0)),
                  pl.BlockSpec((k, tn), lambda i, j: (0, j))],
        out_specs=pl.BlockSpec((tm, tn), lambda i, j: (i, j)),
        out_shape=jax.ShapeDtypeStruct((m, n), out_dtype),
        compiler_params=pltpu.CompilerParams(
            dimension_semantics=("parallel", "arbitrary"), vmem_limit_bytes=VMEM_LIMIT_BYTES),
        name="matmul",
    )(a, w)


def _outproj_kernel(a0_ref, a1_ref, a2_ref, a3_ref, w_ref, o_ref):
    acc = _dot(a0_ref[...], w_ref[0:MIX_W, :])
    acc = acc + _dot(a1_ref[...], w_ref[MIX_W:2 * MIX_W, :])
    acc = acc + _dot(a2_ref[...], w_ref[2 * MIX_W:3 * MIX_W, :])
    acc = acc + _dot(a3_ref[...], w_ref[3 * MIX_W:4 * MIX_W, :])
    o_ref[...] = acc


def _outproj(parts, w, tm=1024, tn=1024):
    m = parts[0].shape[0]
    k, n = w.shape
    assert m % tm == 0 and n % tn == 0 and k == 4 * MIX_W
    a_spec = pl.BlockSpec((tm, MIX_W), lambda i, j: (i, 0))
    return pl.pallas_call(
        _outproj_kernel,
        grid=(m // tm, n // tn),
        in_specs=[a_spec, a_spec, a_spec, a_spec, pl.BlockSpec((k, tn), lambda i, j: (0, j))],
        out_specs=pl.BlockSpec((tm, tn), lambda i, j: (i, j)),
        out_shape=jax.ShapeDtypeStruct((m, n), f32),
        compiler_params=pltpu.CompilerParams(
            dimension_semantics=("parallel", "arbitrary"), vmem_limit_bytes=VMEM_LIMIT_BYTES),
        name="outproj",
    )(*parts, w)


def _matmul_ksplit_kernel(a_ref, w_ref, o_ref, acc_ref):
    k = pl.program_id(2)
    part = _dot(a_ref[...], w_ref[...])

    @pl.when(k == 0)
    def _():
        acc_ref[...] = part

    @pl.when(k != 0)
    def _():
        acc_ref[...] += part

    @pl.when(k == pl.num_programs(2) - 1)
    def _():
        o_ref[...] = acc_ref[...].astype(o_ref.dtype)


def _matmul_ksplit(a, w, out_dtype, tm, tn, tk):
    m, k = a.shape
    n = w.shape[1]
    assert m % tm == 0 and n % tn == 0 and k % tk == 0
    return pl.pallas_call(
        _matmul_ksplit_kernel,
        grid=(m // tm, n // tn, k // tk),
        in_specs=[pl.BlockSpec((tm, tk), lambda i, j, kk: (i, kk)),
                  pl.BlockSpec((tk, tn), lambda i, j, kk: (kk, j))],
        out_specs=pl.BlockSpec((tm, tn), lambda i, j, kk: (i, j)),
        out_shape=jax.ShapeDtypeStruct((m, n), out_dtype),
        scratch_shapes=[pltpu.VMEM((tm, tn), f32)],
        compiler_params=pltpu.CompilerParams(
            dimension_semantics=("parallel", "arbitrary", "arbitrary"), vmem_limit_bytes=VMEM_LIMIT_BYTES),
        name="matmul_ksplit",
    )(a, w)


def _prenorm_kernel(x_ref, w_ref, o_ref):
    x = x_ref[...]
    ms = jnp.mean(x * x, axis=-1, keepdims=True)
    o_ref[...] = (x * lax.rsqrt(ms + RMS_EPS) * w_ref[...]).astype(o_ref.dtype)


def _prenorm(x, w, tm=256):
    m, d = x.shape
    assert m % tm == 0
    return pl.pallas_call(
        _prenorm_kernel,
        grid=(m // tm,),
        in_specs=[pl.BlockSpec((tm, d), lambda i: (i, 0)),
                  pl.BlockSpec((1, d), lambda i: (0, 0))],
        out_specs=pl.BlockSpec((tm, d), lambda i: (i, 0)),
        out_shape=jax.ShapeDtypeStruct((m, d), bf16),
        compiler_params=pltpu.CompilerParams(dimension_semantics=("parallel",)),
        name="prenorm",
    )(x, w.reshape(1, d))


def _postnorm_residual_kernel(x_ref, y_ref, w_ref, o_ref):
    y = y_ref[...]
    ms = jnp.mean(y * y, axis=-1, keepdims=True)
    o_ref[...] = x_ref[...] + y * lax.rsqrt(ms + RMS_EPS) * w_ref[...]


def _postnorm_residual_prenorm_kernel(x_ref, y_ref, w_ref, w2_ref, o_ref, h_ref):
    y = y_ref[...]
    x = x_ref[...] + y * lax.rsqrt(jnp.mean(y * y, axis=-1, keepdims=True) + RMS_EPS) * w_ref[...]
    o_ref[...] = x
    h_ref[...] = (x * lax.rsqrt(jnp.mean(x * x, axis=-1, keepdims=True) + RMS_EPS) * w2_ref[...]).astype(h_ref.dtype)


def _postnorm_residual_prenorm(x, y, w_post, w_pre, tm=256):
    m, d = x.shape
    assert m % tm == 0
    row = pl.BlockSpec((tm, d), lambda i: (i, 0))
    vec = pl.BlockSpec((1, d), lambda i: (0, 0))
    return pl.pallas_call(
        _postnorm_residual_prenorm_kernel,
        grid=(m // tm,),
        in_specs=[row, row, vec, vec],
        out_specs=[row, row],
        out_shape=[jax.ShapeDtypeStruct((m, d), f32), jax.ShapeDtypeStruct((m, d), bf16)],
        compiler_params=pltpu.CompilerParams(dimension_semantics=("parallel",)),
        name="postnorm_residual_prenorm",
    )(x, y, w_post.reshape(1, d), w_pre.reshape(1, d))


def _postnorm_residual(x, y, w, tm=256):
    m, d = x.shape
    assert m % tm == 0
    return pl.pallas_call(
        _postnorm_residual_kernel,
        grid=(m // tm,),
        in_specs=[pl.BlockSpec((tm, d), lambda i: (i, 0)),
                  pl.BlockSpec((tm, d), lambda i: (i, 0)),
                  pl.BlockSpec((1, d), lambda i: (0, 0))],
        out_specs=pl.BlockSpec((tm, d), lambda i: (i, 0)),
        out_shape=jax.ShapeDtypeStruct((m, d), f32),
        compiler_params=pltpu.CompilerParams(dimension_semantics=("parallel",)),
        name="postnorm_residual",
    )(x, y, w.reshape(1, d))


def _ffn_gate_up_kernel(blocks_per_seq, h_ref, wg_ref, wu_ref, cw_ref, o_ref, g_ref, tail_ref):
    i = pl.program_id(0)
    j = pl.program_id(1)
    tm = h_ref.shape[0]

    @pl.when(i % blocks_per_seq == 0)
    def _():
        tail_ref[j] = jnp.zeros(tail_ref.shape[1:], f32)

    h = h_ref[...]
    g = _dot(h, wg_ref[...].astype(bf16))
    g_ref[0:SUBLANES, :] = tail_ref[j]
    g_ref[SUBLANES:SUBLANES + tm, :] = g
    tail_ref[j] = g[tm - SUBLANES:tm, :]
    cw = cw_ref[...]
    conv = (cw[2:3, :] * g
            + cw[1:2, :] * g_ref[SUBLANES - 1:SUBLANES - 1 + tm, :]
            + cw[0:1, :] * g_ref[SUBLANES - 2:SUBLANES - 2 + tm, :])
    u = _dot(h, wu_ref[...].astype(bf16))
    o_ref[...] = (_silu(conv) * u).astype(o_ref.dtype)


def _ffn_gate_up(h, wg, wu, cw, layer, seq, tm=1024, tf=256):
    m, d = h.shape
    f = wg.shape[2]
    assert m % tm == 0 and f % tf == 0 and seq % tm == 0
    return pl.pallas_call(
        functools.partial(_ffn_gate_up_kernel, seq // tm),
        grid=(m // tm, f // tf),
        in_specs=[pl.BlockSpec((tm, d), lambda i, j: (i, 0)),
                  pl.BlockSpec((None, d, tf), lambda i, j: (layer, 0, j)),
                  pl.BlockSpec((None, d, tf), lambda i, j: (layer, 0, j)),
                  pl.BlockSpec((FFN_CONV, tf), lambda i, j: (0, j))],
        out_specs=pl.BlockSpec((tm, tf), lambda i, j: (i, j)),
        out_shape=jax.ShapeDtypeStruct((m, f), bf16),
        scratch_shapes=[pltpu.VMEM((tm + SUBLANES, tf), f32),
                        pltpu.VMEM((f // tf, SUBLANES, tf), f32)],
        compiler_params=pltpu.CompilerParams(
            dimension_semantics=("arbitrary", "arbitrary"), vmem_limit_bytes=VMEM_LIMIT_BYTES),
        name="ffn_gate_up",
    )(h, wg, wu, cw)


def _ssd_kernel(xbc_ref, z_ref, nar_ref, cw_ref, cb_ref, dtb_ref, alog_ref, dsk_ref, nw_ref, exp_ref,
                o_ref, buf_ref, state_ref):
    q = z_ref.shape[0]
    hpg = SSD_HEADS // SSD_GROUPS
    gw = MIX_W // SSD_GROUPS

    @pl.when(pl.program_id(1) == 0)
    def _():
        buf_ref[0:SUBLANES, :] = jnp.zeros((SUBLANES, SSD_XBC), f32)
        state_ref[...] = jnp.zeros_like(state_ref)

    xc = _silu(_causal_conv(xbc_ref[...], buf_ref, cw_ref[...], SSD_CONV) + cb_ref[...])
    x = xc[:, :MIX_W]
    dt = _softplus(nar_ref[...] + dtb_ref[...])
    d_a = dt * (-jnp.exp(alog_ref[...]))
    a_cs = _sel_dot_l(_tri_ones(q), d_a)
    a_cs_t = a_cs.T
    a_end = a_cs[q - 1:q, :]
    expand = exp_ref[...]
    dt_e = _sel_dot_r(dt, expand)
    ea_e = _sel_dot_r(jnp.exp(a_cs), expand)
    de_e = _sel_dot_r(jnp.exp(a_end - a_cs), expand)
    xdt = x * dt_e
    xde = (xdt * de_e).astype(bf16)
    causal = _iota2((q, q), 0) >= _iota2((q, q), 1)
    lane = _iota2((1, LANES), 1)
    y_groups = []
    for g in range(SSD_GROUPS):
        bm = xc[:, MIX_W + g * SSD_STATE:MIX_W + (g + 1) * SSD_STATE].astype(bf16)
        cm = xc[:, MIX_W + (SSD_GROUPS + g) * SSD_STATE:MIX_W + (SSD_GROUPS + g + 1) * SSD_STATE].astype(bf16)
        scores = _dot_nt(cm, bm)
        pieces = []
        for pair in range(hpg // 2):
            c0 = g * gw + pair * LANES
            xp = xdt[:, c0:c0 + LANES]
            acc = None
            for half in range(2):
                h = g * hpg + pair * 2 + half
                seg = a_cs[:, h:h + 1] - a_cs_t[h:h + 1, :]
                p = (scores * jnp.exp(jnp.where(causal, seg, -jnp.inf))).astype(bf16)
                in_half = (lane >= half * SSD_HEAD_DIM) & (lane < (half + 1) * SSD_HEAD_DIM)
                term = _dot(p, jnp.where(in_half, xp, 0.0).astype(bf16))
                acc = term if acc is None else acc + term
            pieces.append(acc)
        y_intra = jnp.concatenate(pieces, axis=1)
        s_prev = state_ref[g]
        y_inter = _dot(cm, s_prev.astype(bf16)) * ea_e[:, g * gw:(g + 1) * gw]
        state_ref[g] = s_prev * ea_e[q - 1:q, g * gw:(g + 1) * gw] + _dot_tn(bm, xde[:, g * gw:(g + 1) * gw])
        y_groups.append(y_intra + y_inter)
    y = jnp.concatenate(y_groups, axis=1) + x * dsk_ref[...]
    y = y * _silu(z_ref[...])
    nw = nw_ref[...]
    out = [_group_rms(y[:, g * gw:(g + 1) * gw], nw[:, g * gw:(g + 1) * gw]) for g in range(SSD_GROUPS)]
    o_ref[...] = jnp.concatenate(out, axis=1).astype(o_ref.dtype)


def _ssd_mixer(proj, bsz, seq, conv_w, conv_b, dt_bias, a_log, d_skip, norm_w):
    q = SSD_Q
    assert seq % q == 0
    nc = seq // q
    head_of_col = jnp.arange(MIX_W) // SSD_HEAD_DIM
    expand = (jnp.arange(LANES)[:, None] == head_of_col[None, :]).astype(bf16)
    row = lambda i, c: (i * nc + c, 0)
    const = lambda i, c: (0, 0)
    return pl.pallas_call(
        _ssd_kernel,
        grid=(bsz, nc),
        in_specs=[pl.BlockSpec((q, SSD_XBC), lambda i, c: (i * nc + c, COL_SSD_XBC // SSD_XBC)),
                  pl.BlockSpec((q, MIX_W), lambda i, c: (i * nc + c, COL_SSD_Z // MIX_W)),
                  pl.BlockSpec((q, LANES), lambda i, c: (i * nc + c, COL_NARROW // LANES)),
                  pl.BlockSpec((SSD_CONV, SSD_XBC), const),
                  pl.BlockSpec((1, SSD_XBC), const),
                  pl.BlockSpec((1, LANES), const),
                  pl.BlockSpec((1, LANES), const),
                  pl.BlockSpec((1, MIX_W), const),
                  pl.BlockSpec((1, MIX_W), const),
                  pl.BlockSpec((LANES, MIX_W), const)],
        out_specs=pl.BlockSpec((q, MIX_W), row),
        out_shape=jax.ShapeDtypeStruct((bsz * seq, MIX_W), bf16),
        scratch_shapes=[pltpu.VMEM((q + SUBLANES, SSD_XBC), f32),
                        pltpu.VMEM((SSD_GROUPS, SSD_STATE, MIX_W // SSD_GROUPS), f32)],
        compiler_params=pltpu.CompilerParams(
            dimension_semantics=("parallel", "arbitrary"), vmem_limit_bytes=VMEM_LIMIT_BYTES),
        name="ssd_mixer",
    )(proj, proj, proj, conv_w.astype(f32), conv_b.reshape(1, SSD_XBC).astype(f32),
      _pad_lanes(dt_bias, LANE_SSD_DT), _pad_lanes(a_log, LANE_SSD_DT),
      jnp.repeat(d_skip.astype(f32), SSD_HEAD_DIM).reshape(1, MIX_W), norm_w.reshape(1, MIX_W).astype(f32), expand)


def _s5_tables(lam_re, lam_im, b_re, b_im, c_re, c_im, log_dt, scan_len):
    hp = lax.Precision.HIGHEST
    t, gb, nb, nh, ns = S5_T, S5_GB, S5_GROUPS // S5_GB, S5_GROUP_CH, S5_STATE
    dt = jnp.exp(log_dt.astype(f32))[:, None]
    ar, ai = lam_re.astype(f32) * dt, lam_im.astype(f32) * dt
    mag = jnp.exp(ar)
    lb_re, lb_im = mag * jnp.cos(ai), mag * jnp.sin(ai)
    den = lam_re * lam_re + lam_im * lam_im
    f_re = ((lb_re - 1.0) * lam_re + lb_im * lam_im) / den
    f_im = (lb_im * lam_re - (lb_re - 1.0) * lam_im) / den
    bb_re = f_re[..., None] * b_re - f_im[..., None] * b_im
    bb_im = f_re[..., None] * b_im + f_im[..., None] * b_re
    d = jnp.arange(t + 1, dtype=f32)[:, None, None]
    pw_mag = jnp.exp(d * ar)
    pw_re, pw_im = pw_mag * jnp.cos(d * ai), pw_mag * jnp.sin(d * ai)
    cr, ci = c_re.astype(f32), c_im.astype(f32)
    m_re = cr[None] * pw_re[:, :, None, :] - ci[None] * pw_im[:, :, None, :]
    m_im = cr[None] * pw_im[:, :, None, :] + ci[None] * pw_re[:, :, None, :]
    kern = (jnp.einsum('dgop,gpi->dgoi', m_re[:t], bb_re, precision=hp)
            - jnp.einsum('dgop,gpi->dgoi', m_im[:t], bb_im, precision=hp))
    eye = jnp.eye(gb, dtype=f32)

    def block_diag(x):
        shape_eye = [1] * (x.ndim + 1)
        shape_eye[2] = gb
        shape_eye[-2] = gb
        return x[..., None, :] * eye.reshape(shape_eye)

    lagk = kern.reshape(t, nb, gb, nh, nh).transpose(1, 0, 2, 4, 3)
    lagk = block_diag(lagk).reshape(nb, t, gb * nh, gb * nh)
    d_rev = jnp.arange(t - 1, -1, -1).astype(f32)[:, None, None]
    rev_mag = jnp.exp(d_rev * ar)
    rev_re, rev_im = rev_mag * jnp.cos(d_rev * ai), rev_mag * jnp.sin(d_rev * ai)
    v_re = rev_re[..., None] * bb_re[None] - rev_im[..., None] * bb_im[None]
    v_im = rev_re[..., None] * bb_im[None] + rev_im[..., None] * bb_re[None]
    vmat = jnp.stack([v_re, v_im], axis=0).reshape(2, t, nb, gb, ns, nh)
    vmat = vmat.transpose(2, 1, 3, 5, 0, 4)
    vmat = block_diag(vmat).reshape(nb, t * gb * nh, 2 * gb * ns)
    wmat = jnp.stack([m_re[1:], -m_im[1:]], axis=0).reshape(2, t, nb, gb, nh, ns)
    wmat = wmat.transpose(2, 0, 3, 5, 1, 4)
    wmat = block_diag(wmat).reshape(nb, 2 * gb * ns, t * gb * nh)
    a_re, a_im = pw_re[t].reshape(nb, gb * ns), pw_im[t].reshape(nb, gb * ns)
    c1, c2 = [], []
    for _ in range(max(1, (scan_len - 1).bit_length())):
        c1.append(jnp.concatenate([a_re, a_re], axis=-1))
        c2.append(jnp.concatenate([-a_im, a_im], axis=-1))
        a_re, a_im = a_re * a_re - a_im * a_im, 2.0 * a_re * a_im
    return lagk.astype(bf16), vmat.astype(bf16), wmat.astype(bf16), jnp.stack(c1, axis=1), jnp.stack(c2, axis=1)


def _s5_scan_kernel(u_ref, lagk_ref, v_ref, w_ref, c1_ref, c2_ref, o_ref, toep_ref):
    t, cps = u_ref.shape[0], u_ref.shape[1]
    half = c1_ref.shape[1] // 2

    @pl.when(pl.program_id(1) == 0)
    def _():
        for j in range(t):
            for tt in range(t):
                blk = lagk_ref[tt - j] if tt >= j else jnp.zeros((LANES, LANES), bf16)
                toep_ref[j * LANES:(j + 1) * LANES, tt * LANES:(tt + 1) * LANES] = blk

    x_all = jnp.concatenate([u_ref[j].astype(bf16) for j in range(t)], axis=1)
    z = _dot(x_all, v_ref[...])
    pos = _iota2((cps, 1), 0)
    c1, c2 = c1_ref[...], c2_ref[...]
    s = z
    for k in range(c1.shape[0]):
        step = 1 << k
        if step >= cps:
            break
        sh = jnp.where(pos >= step, pltpu.roll(s, step, 0), 0.0)
        s = s + sh * c1[k:k + 1, :] + pltpu.roll(sh, half, 1) * c2[k:k + 1, :]
    s_in = jnp.where(pos >= 1, pltpu.roll(s, 1, 0), 0.0)
    y = _dot(x_all, toep_ref[...]) + _dot(s_in.astype(bf16), w_ref[...])
    for j in range(t):
        o_ref[j] = y[:, j * LANES:(j + 1) * LANES]


def _s5_glu_kernel(y_ref, u_ref, d_ref, gw_ref, gb_ref, o_ref):
    y = y_ref[...] + d_ref[...] * u_ref[...]
    y = 0.5 * y * (1.0 + jnp.tanh(math.sqrt(2.0 / math.pi) * (y + 0.044715 * (y * y * y))))
    gate = _dot(y.astype(bf16), gw_ref[...]) + gb_ref[...]
    o_ref[...] = (y * jax.nn.sigmoid(gate)).astype(o_ref.dtype)


def _chunk_major(a, t):
    rows, w = a.shape
    return a.reshape(rows // t, t, w).transpose(1, 0, 2)


def _token_major(a3):
    t, chunks, w = a3.shape
    return a3.transpose(1, 0, 2).reshape(chunks * t, w)


def _s5_mixer(u3, bsz, seq, lam_re, lam_im, b_re, b_im, c_re, c_im, d_skip, log_dt, glu_w, glu_b, tm=512):
    t = S5_T
    assert seq % t == 0 and u3.shape == (t, bsz * seq // t, MIX_W)
    m = bsz * seq
    cps = seq // t
    nb = S5_GROUPS // S5_GB
    lagk, vmat, wmat, c1, c2 = _s5_tables(lam_re, lam_im, b_re, b_im, c_re, c_im, log_dt, cps)
    nk = c1.shape[1]
    tw, sw = t * LANES, 2 * S5_GB * S5_STATE
    per_block = lambda blk, b: (blk, 0, 0)
    seq_block = pl.BlockSpec((t, cps, LANES), lambda blk, b: (0, b, blk))
    y3 = pl.pallas_call(
        _s5_scan_kernel,
        grid=(nb, bsz),
        in_specs=[seq_block,
                  pl.BlockSpec((None, t, LANES, LANES), lambda blk, b: (blk, 0, 0, 0)),
                  pl.BlockSpec((None, tw, sw), per_block),
                  pl.BlockSpec((None, sw, tw), per_block),
                  pl.BlockSpec((None, nk, sw), per_block),
                  pl.BlockSpec((None, nk, sw), per_block)],
        out_specs=seq_block,
        out_shape=jax.ShapeDtypeStruct(u3.shape, f32),
        scratch_shapes=[pltpu.VMEM((tw, tw), bf16)],
        compiler_params=pltpu.CompilerParams(
            dimension_semantics=("arbitrary", "arbitrary"), vmem_limit_bytes=VMEM_LIMIT_BYTES),
        name="s5_scan",
    )(u3, lagk, vmat, wmat, c1, c2)
    assert m % tm == 0
    row_block = pl.BlockSpec((tm, MIX_W), lambda i: (i, 0))
    const = lambda i: (0, 0)
    out = pl.pallas_call(
        _s5_glu_kernel,
        grid=(m // tm,),
        in_specs=[row_block, row_block,
                  pl.BlockSpec((1, MIX_W), const),
                  pl.BlockSpec((MIX_W, MIX_W), const),
                  pl.BlockSpec((1, MIX_W), const)],
        out_specs=row_block,
        out_shape=jax.ShapeDtypeStruct((m, MIX_W), bf16),
        compiler_params=pltpu.CompilerParams(dimension_semantics=("parallel",)),
        name="s5_glu",
    )(y3.reshape(m, MIX_W), u3.reshape(m, MIX_W), d_skip.reshape(1, MIX_W).astype(f32), glu_w.astype(bf16),
      glu_b.reshape(1, MIX_W).astype(f32))
    return _token_major(out.reshape(t, m // t, MIX_W))


def _gla_kernel(qk_ref, v_ref, r_ref, nar_ref, w2_ref, gb_ref, nw_ref, sel_ref, o_ref, state_ref):
    q = v_ref.shape[0]
    levels = q.bit_length() - 1
    dkw = GLA_HEADS * GLA_DK

    @pl.when(pl.program_id(1) == 0)
    def _():
        state_ref[...] = jnp.zeros_like(state_ref)

    gate = _dot(nar_ref[...].astype(bf16), w2_ref[...]) + gb_ref[...]
    log_a = -_softplus(-gate) * (1.0 / GLA_GATE_NORM)
    b = _sel_dot_l(_tri_ones(q), log_a)
    refs = _sel_dot_l(sel_ref[...], b)
    qk = qk_ref[...]
    qs = qk[:, :dkw] * (GLA_DK ** -0.5)
    ks = qk[:, dkw:]
    v = v_ref[...]
    rgate = _silu(r_ref[...])
    nw = nw_ref[...]
    row = _iota2((q, 1), 0)
    ri, ci = _iota2((q, q), 0), _iota2((q, q), 1)
    eb = jnp.exp(b)
    b_end = b[q - 1:q, :]
    k_end = (ks * jnp.exp(b_end - b)).astype(bf16)
    out = []
    for h in range(GLA_HEADS):
        sl = slice(h * GLA_DK, (h + 1) * GLA_DK)
        qh, kh, bh = qs[:, sl], ks[:, sl], b[:, sl]
        vh = v[:, h * GLA_DV:(h + 1) * GLA_DV].astype(bf16)
        scores = jnp.where(ri == ci, jnp.sum(qh * kh, axis=-1, keepdims=True), 0.0)
        for lv in range(levels):
            s = 1 << lv
            upper = ((row >> lv) & 1) == 1
            rl = refs[lv * q:(lv + 1) * q, sl]
            qt = (qh * jnp.exp(jnp.where(upper, bh - rl, -jnp.inf))).astype(bf16)
            kt = (kh * jnp.exp(jnp.where(upper, -jnp.inf, rl - bh))).astype(bf16)
            same_block = (ri >> (lv + 1)) == (ci >> (lv + 1))
            scores = scores + jnp.where(same_block, _dot_nt(qt, kt), 0.0)
        s_prev = state_ref[h]
        o = _dot(scores.astype(bf16), vh) + _dot_nt((qh * eb[:, sl]).astype(bf16), s_prev.astype(bf16))
        state_ref[h] = s_prev * jnp.exp(b_end[:, sl]) + _dot_tn(vh, k_end[:, sl])
        o = _group_rms(o, nw) * rgate[:, h * GLA_DV:(h + 1) * GLA_DV]
        out.append(o)
    o_ref[...] = jnp.concatenate(out, axis=1).astype(o_ref.dtype)


def _gla_mixer(proj, bsz, seq, gate_w2, gate_b, norm_w):
    q = GLA_Q
    assert seq % q == 0 and q & (q - 1) == 0
    nc = seq // q
    levels = q.bit_length() - 1
    i = jnp.arange(q)
    sel = jnp.concatenate(
        [(((i // (2 << lv)) * (2 << lv) + (1 << lv))[:, None] == i[None, :]) for lv in range(levels)], axis=0).astype(bf16)
    w2 = jnp.zeros((LANES, GLA_HEADS * GLA_DK), f32).at[LANE_GLA_G:LANE_GLA_G + GLA_GATE_RANK].set(gate_w2.astype(f32))
    const = lambda i, c: (0, 0)
    blk = lambda col: pl.BlockSpec((q, MIX_W), lambda i, c: (i * nc + c, col // MIX_W))
    return pl.pallas_call(
        _gla_kernel,
        grid=(bsz, nc),
        in_specs=[blk(COL_GLA_QK), blk(COL_GLA_V), blk(COL_GLA_R),
                  pl.BlockSpec((q, LANES), lambda i, c: (i * nc + c, COL_NARROW // LANES)),
                  pl.BlockSpec((LANES, GLA_HEADS * GLA_DK), const),
                  pl.BlockSpec((1, GLA_HEADS * GLA_DK), const),
                  pl.BlockSpec((1, GLA_DV), const),
                  pl.BlockSpec((levels * q, q), const)],
        out_specs=pl.BlockSpec((q, MIX_W), lambda i, c: (i * nc + c, 0)),
        out_shape=jax.ShapeDtypeStruct((bsz * seq, MIX_W), bf16),
        scratch_shapes=[pltpu.VMEM((GLA_HEADS, GLA_DV, GLA_DK), f32)],
        compiler_params=pltpu.CompilerParams(
            dimension_semantics=("parallel", "arbitrary"), vmem_limit_bytes=VMEM_LIMIT_BYTES),
        name="gla_mixer",
    )(proj, proj, proj, proj, w2.astype(bf16), gate_b.reshape(1, -1).astype(f32),
      norm_w.reshape(1, GLA_DV).astype(f32), sel)


def _gdn_kernel(qkv_ref, z_ref, nar_ref, cw_ref, alog_ref, dtb_ref, nw_ref, o_ref, buf_ref, state_ref):
    c = z_ref.shape[0]
    hw = GDN_HEADS * GDN_DK

    @pl.when(pl.program_id(1) == 0)
    def _():
        buf_ref[0:SUBLANES, :] = jnp.zeros((SUBLANES, 3 * MIX_W), f32)
        state_ref[...] = jnp.zeros_like(state_ref)

    qkv = _silu(_causal_conv(qkv_ref[...], buf_ref, cw_ref[...], GDN_CONV))
    nar = nar_ref[...]
    beta = jax.nn.sigmoid(nar)
    g = -jnp.exp(alog_ref[...]) * _softplus(nar + dtb_ref[...])
    gc = _sel_dot_l(_tri_ones(c), g)
    gc_t = gc.T
    eg = jnp.exp(gc)
    eg_rev = jnp.exp(gc[c - 1:c, :] - gc)
    ri, ci = _iota2((c, c), 0), _iota2((c, c), 1)
    z = z_ref[...]
    nw = nw_ref[...]
    heads = range(GDN_HEADS)
    qn, kn, kb16, kh16, decay, egc, x_cur = [], [], [], [], [], [], []
    for h in heads:
        qh = qkv[:, h * GDN_DK:(h + 1) * GDN_DK]
        kh = qkv[:, hw + h * GDN_DK:hw + (h + 1) * GDN_DK]
        vh = qkv[:, 2 * hw + h * GDN_DV:2 * hw + (h + 1) * GDN_DV]
        qn.append(qh * lax.rsqrt(jnp.sum(qh * qh, axis=-1, keepdims=True) + 1e-6) * (GDN_DK ** -0.5))
        kh = kh * lax.rsqrt(jnp.sum(kh * kh, axis=-1, keepdims=True) + 1e-6)
        kn.append(kh)
        bh = beta[:, LANE_GDN_B + h:LANE_GDN_B + h + 1]
        la = LANE_GDN_A + h
        egc.append(eg[:, la:la + 1])
        decay.append(jnp.exp(jnp.where(ri >= ci, gc[:, la:la + 1] - gc_t[la:la + 1, :], -jnp.inf)))
        kb = kh * bh
        kb16.append(kb.astype(bf16))
        kh16.append(kh.astype(bf16))
        x_cur.append(jnp.concatenate([vh * bh, kb * egc[h]], axis=1))
    p_cur = [-jnp.where(ri > ci, _dot_nt(kb16[h], kh16[h]) * decay[h], 0.0) for h in heads]
    n_fac = c.bit_length() - 1
    for k in range(n_fac):
        p_hi = [p.astype(bf16) for p in p_cur]
        p_lo = [(p - ph.astype(f32)).astype(bf16) for p, ph in zip(p_cur, p_hi)]
        x_hi = [x.astype(bf16) for x in x_cur]
        x_lo = [(x - xh.astype(f32)).astype(bf16) for x, xh in zip(x_cur, x_hi)]
        px = [_dot(p_hi[h], x_hi[h]) + _dot(p_hi[h], x_lo[h]) + _dot(p_lo[h], x_hi[h]) for h in heads]
        x_cur = [x_cur[h] + px[h] for h in heads]
        if k + 1 < n_fac:
            p_cur = [_dot(p_hi[h], p_hi[h]) + _dot(p_hi[h], p_lo[h]) + _dot(p_lo[h], p_hi[h]) for h in heads]
    s_prev = [state_ref[h] for h in heads]
    s16 = [s.astype(bf16) for s in s_prev]
    v_new = [(x_cur[h][:, :GDN_DV] - _dot(x_cur[h][:, GDN_DV:].astype(bf16), s16[h])).astype(bf16) for h in heads]
    attn = [(_dot_nt(qn[h].astype(bf16), kh16[h]) * decay[h]).astype(bf16) for h in heads]
    o = [_dot((qn[h] * egc[h]).astype(bf16), s16[h]) + _dot(attn[h], v_new[h]) for h in heads]
    for h in heads:
        la = LANE_GDN_A + h
        k_end = (kn[h] * eg_rev[:, la:la + 1]).astype(bf16)
        state_ref[h] = s_prev[h] * eg[c - 1:c, la:la + 1] + _dot_tn(k_end, v_new[h])
    out = [_group_rms(o[h], nw) * _silu(z[:, h * GDN_DV:(h + 1) * GDN_DV]) for h in heads]
    o_ref[...] = jnp.concatenate(out, axis=1).astype(o_ref.dtype)


def _gdn_mixer(proj, bsz, seq, conv_w, a_log, dt_bias, norm_w):
    c = GDN_C
    assert seq % c == 0 and c & (c - 1) == 0
    nc = seq // c
    const = lambda i, cc: (0, 0)
    return pl.pallas_call(
        _gdn_kernel,
        grid=(bsz, nc),
        in_specs=[pl.BlockSpec((c, 3 * MIX_W), lambda i, cc: (i * nc + cc, COL_GDN_QKV // (3 * MIX_W))),
                  pl.BlockSpec((c, MIX_W), lambda i, cc: (i * nc + cc, COL_GDN_Z // MIX_W)),
                  pl.BlockSpec((c, LANES), lambda i, cc: (i * nc + cc, COL_NARROW // LANES)),
                  pl.BlockSpec((GDN_CONV, 3 * MIX_W), const),
                  pl.BlockSpec((1, LANES), const),
                  pl.BlockSpec((1, LANES), const),
                  pl.BlockSpec((1, GDN_DV), const)],
        out_specs=pl.BlockSpec((c, MIX_W), lambda i, cc: (i * nc + cc, 0)),
        out_shape=jax.ShapeDtypeStruct((bsz * seq, MIX_W), bf16),
        scratch_shapes=[pltpu.VMEM((c + SUBLANES, 3 * MIX_W), f32),
                        pltpu.VMEM((GDN_HEADS, GDN_DK, GDN_DV), f32)],
        compiler_params=pltpu.CompilerParams(
            dimension_semantics=("parallel", "arbitrary"), vmem_limit_bytes=VMEM_LIMIT_BYTES),
        name="gdn_mixer",
    )(proj, proj, proj, conv_w.astype(f32), _pad_lanes(a_log, LANE_GDN_A), _pad_lanes(dt_bias, LANE_GDN_A),
      norm_w.reshape(1, GDN_DV).astype(f32))


def kernel(x, norm_mix_pre, norm_mix_post, norm_ffn_pre, norm_ffn_post, w_in, w_out, ssd_conv_w, ssd_conv_b, ssd_dt_bias, ssd_a_log, ssd_d, ssd_norm, s5_lambda_re, s5_lambda_im, s5_b_re, s5_b_im, s5_c_re, s5_c_im, s5_d, s5_log_dt, s5_glu_w, s5_glu_b, gla_gate_w2, gla_gate_b, gla_norm, gdn_conv_w, gdn_a_log, gdn_dt_bias, gdn_norm, ffn_w_gate, ffn_w_up, ffn_conv_w, ffn_w_down):
    bsz, seq, d = x.shape
    m = bsz * seq
    xf = x.reshape(m, d)
    h = _prenorm(xf, norm_mix_pre[0])
    for l in range(DEPTH):
        proj = _matmul(h, _permute_w_in(w_in[l]), f32, 1024, PROJ_TN)
        y_ssd = _ssd_mixer(proj, bsz, seq, ssd_conv_w[l], ssd_conv_b[l], ssd_dt_bias[l], ssd_a_log[l],
                           ssd_d[l], ssd_norm[l])
        h_cm = _chunk_major(h, S5_T).reshape(m, d)
        u3 = _matmul(h_cm, _s5_w_in(w_in[l]), f32, 1024, 1024).reshape(S5_T, m // S5_T, MIX_W)
        y_s5 = _s5_mixer(u3, bsz, seq, s5_lambda_re[l], s5_lambda_im[l], s5_b_re[l], s5_b_im[l],
                         s5_c_re[l], s5_c_im[l], s5_d[l], s5_log_dt[l], s5_glu_w[l], s5_glu_b[l])
        y_gla = _gla_mixer(proj, bsz, seq, gla_gate_w2[l], gla_gate_b[l], gla_norm[l])
        y_gdn = _gdn_mixer(proj, bsz, seq, gdn_conv_w[l], gdn_a_log[l], gdn_dt_bias[l], gdn_norm[l])
        mixed = _outproj([y_ssd, y_s5, y_gla, y_gdn], w_out[l].astype(bf16))
        xf, h = _postnorm_residual_prenorm(xf, mixed, norm_mix_post[l], norm_ffn_pre[l])
        act = _ffn_gate_up(h, ffn_w_gate, ffn_w_up, ffn_conv_w[l], l, seq)
        down = _matmul_ksplit(act, ffn_w_down[l].astype(bf16), f32, 512, 1024, D_FF // 2)
        if l + 1 < DEPTH:
            xf, h = _postnorm_residual_prenorm(xf, down, norm_ffn_post[l], norm_mix_pre[l + 1])
        else:
            xf = _postnorm_residual(xf, down, norm_ffn_post[l])
    return xf.reshape(bsz, seq, d)
```

```python
import functools
import math

import jax
import jax.numpy as jnp
from jax import lax
from jax.experimental import pallas as pl
from jax.experimental.pallas import tpu as pltpu

D_MODEL = 4096
DEPTH = 2
MIX_W = D_MODEL // 4
RMS_EPS = 1e-6
D_FF = 11008
FFN_CONV = 3

SSD_HEAD_DIM = 64
SSD_HEADS = MIX_W // SSD_HEAD_DIM
SSD_STATE = 128
SSD_GROUPS = 2
SSD_CONV = 4
SSD_XBC = MIX_W + 2 * SSD_GROUPS * SSD_STATE
SSD_Q = 128

S5_GROUP_CH = 16
S5_GROUPS = MIX_W // S5_GROUP_CH
S5_STATE = 64
S5_T = 16
S5_GB = 8

GLA_HEADS = 4
GLA_DK = MIX_W // (2 * GLA_HEADS)
GLA_DV = MIX_W // GLA_HEADS
GLA_GATE_RANK = 16
GLA_GATE_NORM = 16.0
GLA_Q = 128

GDN_HEADS = 8
GDN_DK = MIX_W // GDN_HEADS
GDN_DV = MIX_W // GDN_HEADS
GDN_CONV = 4
GDN_C = 64

SPLIT_SIZES = (
    MIX_W, SSD_XBC, SSD_HEADS,
    MIX_W,
    GLA_HEADS * GLA_DK, GLA_HEADS * GLA_DK, MIX_W, MIX_W, GLA_GATE_RANK,
    GDN_HEADS * GDN_DK, GDN_HEADS * GDN_DK, GDN_HEADS * GDN_DV, MIX_W, GDN_HEADS, GDN_HEADS,
)
D_IN_PROJ = sum(SPLIT_SIZES)
(SEG_SSD_Z, SEG_SSD_XBC, SEG_SSD_DT, SEG_S5_U, SEG_GLA_Q, SEG_GLA_K, SEG_GLA_V, SEG_GLA_R, SEG_GLA_G,
 SEG_GDN_Q, SEG_GDN_K, SEG_GDN_V, SEG_GDN_Z, SEG_GDN_B, SEG_GDN_A) = range(len(SPLIT_SIZES))

LANES = 128
SUBLANES = 8
VMEM_LIMIT_BYTES = 56 * 1024 * 1024

N_PROJ_PAD = 10240
PROJ_TN = 1024
COL_GDN_QKV = 0
COL_SSD_XBC = 3072
COL_NARROW = 4608
COL_GLA_QK = 5120
COL_GLA_V = 6144
COL_GLA_R = 7168
COL_GDN_Z = 8192
COL_SSD_Z = 9216
LANE_SSD_DT = 0
LANE_GLA_G = 16
LANE_GDN_B = 32
LANE_GDN_A = 40
_DST_OFF = {
    SEG_GDN_Q: COL_GDN_QKV, SEG_GDN_K: COL_GDN_QKV + 1024, SEG_GDN_V: COL_GDN_QKV + 2048,
    SEG_SSD_XBC: COL_SSD_XBC,
    SEG_SSD_DT: COL_NARROW + LANE_SSD_DT, SEG_GLA_G: COL_NARROW + LANE_GLA_G,
    SEG_GDN_B: COL_NARROW + LANE_GDN_B, SEG_GDN_A: COL_NARROW + LANE_GDN_A,
    SEG_GLA_Q: COL_GLA_QK, SEG_GLA_K: COL_GLA_QK + 512, SEG_GLA_V: COL_GLA_V, SEG_GLA_R: COL_GLA_R,
    SEG_GDN_Z: COL_GDN_Z, SEG_SSD_Z: COL_SSD_Z,
}

f32 = jnp.float32
bf16 = jnp.bfloat16


def _permute_w_in(w):
    src = [0]
    for s in SPLIT_SIZES:
        src.append(src[-1] + s)
    order = sorted(_DST_OFF, key=_DST_OFF.get)
    parts, pos = [], 0
    for i in order:
        if _DST_OFF[i] > pos:
            parts.append(jnp.zeros((w.shape[0], _DST_OFF[i] - pos), w.dtype))
        parts.append(w[:, src[i]:src[i + 1]])
        pos = _DST_OFF[i] + SPLIT_SIZES[i]
    if N_PROJ_PAD > pos:
        parts.append(jnp.zeros((w.shape[0], N_PROJ_PAD - pos), w.dtype))
    return jnp.concatenate(parts, axis=1).astype(bf16)


def _s5_w_in(w):
    off = sum(SPLIT_SIZES[:SEG_S5_U])
    return w[:, off:off + MIX_W].astype(bf16)


def _pad_lanes(vec, offset):
    return jnp.zeros((1, LANES), f32).at[0, offset:offset + vec.shape[0]].set(vec.astype(f32))


def _dot(a, b):
    return jnp.dot(a, b, preferred_element_type=f32)


def _dot_nt(a, b):
    return lax.dot_general(a, b, (((1,), (1,)), ((), ())), preferred_element_type=f32)


def _dot_tn(a, b):
    return lax.dot_general(a, b, (((0,), (0,)), ((), ())), preferred_element_type=f32)


def _split3(x):
    hi = x.astype(bf16)
    r1 = x - hi.astype(f32)
    mid = r1.astype(bf16)
    lo = (r1 - mid.astype(f32)).astype(bf16)
    return hi, mid, lo


def _sel_dot_l(sel, x):
    hi, mid, lo = _split3(x)
    return _dot(sel, hi) + _dot(sel, mid) + _dot(sel, lo)


def _sel_dot_r(x, sel):
    hi, mid, lo = _split3(x)
    return _dot(hi, sel) + _dot(mid, sel) + _dot(lo, sel)


def _softplus(x):
    return jnp.maximum(x, 0.0) + jnp.log1p(jnp.exp(-jnp.abs(x)))


def _silu(x):
    return x * jax.nn.sigmoid(x)


def _iota2(shape, dim):
    return lax.broadcasted_iota(jnp.int32, shape, dim)


def _tri_ones(n):
    return (_iota2((n, n), 0) >= _iota2((n, n), 1)).astype(bf16)


def _causal_conv(x, buf_ref, cw, kw):
    q = x.shape[0]
    buf_ref[SUBLANES:SUBLANES + q, :] = x
    acc = cw[kw - 1:kw, :] * x
    for k in range(kw - 1):
        s = SUBLANES - (kw - 1) + k
        acc = acc + cw[k:k + 1, :] * buf_ref[s:s + q, :]
    buf_ref[0:SUBLANES, :] = x[q - SUBLANES:q, :]
    return acc


def _group_rms(y, nw):
    ms = jnp.mean(y * y, axis=-1, keepdims=True)
    return y * lax.rsqrt(ms + RMS_EPS) * nw


def _matmul_kernel(a_ref, w_ref, o_ref):
    o_ref[...] = _dot(a_ref[...], w_ref[...]).astype(o_ref.dtype)


def _matmul(a, w, out_dtype, tm, tn):
    m, k = a.shape
    n = w.shape[1]
    assert m % tm == 0 and n % tn == 0
    return pl.pallas_call(
        _matmul_kernel,
        grid=(m // tm, n // tn),
        in_specs=[pl.BlockSpec((tm, k), lambda i, j: (i, 0)),
                  pl.BlockSpec((k, tn), lambda i, j: (0, j))],
        out_specs=pl.BlockSpec((tm, tn), lambda i, j: (i, j)),
        out_shape=jax.ShapeDtypeStruct((m, n), out_dtype),
        compiler_params=pltpu.CompilerParams(
            dimension_semantics=("parallel", "arbitrary"), vmem_limit_bytes=VMEM_LIMIT_BYTES),
        name="matmul",
    )(a, w)


def _outproj_kernel(a0_ref, a1_ref, a2_ref, a3_ref, w_ref, o_ref):
    acc = _dot(a0_ref[...], w_ref[0:MIX_W, :])
    acc = acc + _dot(a1_ref[...], w_ref[MIX_W:2 * MIX_W, :])
    acc = acc + _dot(a2_ref[...], w_ref[2 * MIX_W:3 * MIX_W, :])
    acc = acc + _dot(a3_ref[...], w_ref[3 * MIX_W:4 * MIX_W, :])
    o_ref[...] = acc


def _outproj(parts, w, tm=1024, tn=1024):
    m = parts[0].shape[0]
    k, n = w.shape
    assert m % tm == 0 and n % tn == 0 and k == 4 * MIX_W
    a_spec = pl.BlockSpec((tm, MIX_W), lambda i, j: (i, 0))
    return pl.pallas_call(
        _outproj_kernel,
        grid=(m // tm, n // tn),
        in_specs=[a_spec, a_spec, a_spec, a_spec, pl.BlockSpec((k, tn), lambda i, j: (0, j))],
        out_specs=pl.BlockSpec((tm, tn), lambda i, j: (i, j)),
        out_shape=jax.ShapeDtypeStruct((m, n), f32),
        compiler_params=pltpu.CompilerParams(
            dimension_semantics=("parallel", "arbitrary"), vmem_limit_bytes=VMEM_LIMIT_BYTES),
        name="outproj",
    )(*parts, w)


def _matmul_ksplit_kernel(a_ref, w_ref, o_ref, acc_ref):
    k = pl.program_id(2)
    part = _dot(a_ref[...], w_ref[...])

    @pl.when(k == 0)
    def _():
        acc_ref[...] = part

    @pl.when(k != 0)
    def _():
        acc_ref[...] += part

    @pl.when(k == pl.num_programs(2) - 1)
    def _():
        o_ref[...] = acc_ref[...].astype(o_ref.dtype)


def _matmul_ksplit(a, w, out_dtype, tm, tn, tk):
    m, k = a.shape
    n = w.shape[1]
    assert m % tm == 0 and n % tn == 0 and k % tk == 0
    return pl.pallas_call(
        _matmul_ksplit_kernel,
        grid=(m // tm, n // tn, k // tk),
        in_specs=[pl.BlockSpec((tm, tk), lambda i, j, kk: (i, kk)),
                  pl.BlockSpec((tk, tn), lambda i, j, kk: (kk, j))],
        out_specs=pl.BlockSpec((tm, tn), lambda i, j, kk: (i, j)),
        out_shape=jax.ShapeDtypeStruct((m, n), out_dtype),
        scratch_shapes=[pltpu.VMEM((tm, tn), f32)],
        compiler_params=pltpu.CompilerParams(
            dimension_semantics=("parallel", "arbitrary", "arbitrary"), vmem_limit_bytes=VMEM_LIMIT_BYTES),
        name="matmul_ksplit",
    )(a, w)


def _prenorm_kernel(x_ref, w_ref, o_ref):
    x = x_ref[...]
    ms = jnp.mean(x * x, axis=-1, keepdims=True)
    o_ref[...] = (x * lax.rsqrt(ms + RMS_EPS) * w_ref[...]).astype(o_ref.dtype)


def _prenorm(x, w, tm=256):
    m, d = x.shape
    assert m % tm == 0
    return pl.pallas_call(
        _prenorm_kernel,
        grid=(m // tm,),
        in_specs=[pl.BlockSpec((tm, d), lambda i: (i, 0)),
                  pl.BlockSpec((1, d), lambda i: (0, 0))],
        out_specs=pl.BlockSpec((tm, d), lambda i: (i, 0)),
        out_shape=jax.ShapeDtypeStruct((m, d), bf16),
        compiler_params=pltpu.CompilerParams(dimension_semantics=("parallel",)),
        name="prenorm",
    )(x, w.reshape(1, d))


def _postnorm_residual_kernel(x_ref, y_ref, w_ref, o_ref):
    y = y_ref[...]
    ms = jnp.mean(y * y, axis=-1, keepdims=True)
    o_ref[...] = x_ref[...] + y * lax.rsqrt(ms + RMS_EPS) * w_ref[...]


def _postnorm_residual_prenorm_kernel(x_ref, y_ref, w_ref, w2_ref, o_ref, h_ref):
    y = y_ref[...]
    x = x_ref[...] + y * lax.rsqrt(jnp.mean(y * y, axis=-1, keepdims=True) + RMS_EPS) * w_ref[...]
    o_ref[...] = x
    h_ref[...] = (x * lax.rsqrt(jnp.mean(x * x, axis=-1, keepdims=True) + RMS_EPS) * w2_ref[...]).astype(h_ref.dtype)


def _postnorm_residual_prenorm(x, y, w_post, w_pre, tm=256):
    m, d = x.shape
    assert m % tm == 0
    row = pl.BlockSpec((tm, d), lambda i: (i, 0))
    vec = pl.BlockSpec((1, d), lambda i: (0, 0))
    return pl.pallas_call(
        _postnorm_residual_prenorm_kernel,
        grid=(m // tm,),
        in_specs=[row, row, vec, vec],
        out_specs=[row, row],
        out_shape=[jax.ShapeDtypeStruct((m, d), f32), jax.ShapeDtypeStruct((m, d), bf16)],
        compiler_params=pltpu.CompilerParams(dimension_semantics=("parallel",)),
        name="postnorm_residual_prenorm",
    )(x, y, w_post.reshape(1, d), w_pre.reshape(1, d))


def _postnorm_residual(x, y, w, tm=256):
    m, d = x.shape
    assert m % tm == 0
    return pl.pallas_call(
        _postnorm_residual_kernel,
        grid=(m // tm,),
        in_specs=[pl.BlockSpec((tm, d), lambda i: (i, 0)),
                  pl.BlockSpec((tm, d), lambda i: (i, 0)),
                  pl.BlockSpec((1, d), lambda i: (0, 0))],
        out_specs=pl.BlockSpec((tm, d), lambda i: (i, 0)),
        out_shape=jax.ShapeDtypeStruct((m, d), f32),
        compiler_params=pltpu.CompilerParams(dimension_semantics=("parallel",)),
        name="postnorm_residual",
    )(x, y, w.reshape(1, d))


def _ffn_gate_up_kernel(blocks_per_seq, h_ref, wg_ref, wu_ref, cw_ref, o_ref, g_ref, tail_ref):
    i = pl.program_id(0)
    j = pl.program_id(1)
    tm = h_ref.shape[0]

    @pl.when(i % blocks_per_seq == 0)
    def _():
        tail_ref[j] = jnp.zeros(tail_ref.shape[1:], f32)

    h = h_ref[...]
    g = _dot(h, wg_ref[...].astype(bf16))
    g_ref[0:SUBLANES, :] = tail_ref[j]
    g_ref[SUBLANES:SUBLANES + tm, :] = g
    tail_ref[j] = g[tm - SUBLANES:tm, :]
    cw = cw_ref[...]
    conv = (cw[2:3, :] * g
            + cw[1:2, :] * g_ref[SUBLANES - 1:SUBLANES - 1 + tm, :]
            + cw[0:1, :] * g_ref[SUBLANES - 2:SUBLANES - 2 + tm, :])
    u = _dot(h, wu_ref[...].astype(bf16))
    o_ref[...] = (_silu(conv) * u).astype(o_ref.dtype)


def _ffn_gate_up(h, wg, wu, cw, layer, seq, tm=1024, tf=256):
    m, d = h.shape
    f = wg.shape[2]
    assert m % tm == 0 and f % tf == 0 and seq % tm == 0
    return pl.pallas_call(
        functools.partial(_ffn_gate_up_kernel, seq // tm),
        grid=(m // tm, f // tf),
        in_specs=[pl.BlockSpec((tm, d), lambda i, j: (i, 0)),
                  pl.BlockSpec((None, d, tf), lambda i, j: (layer, 0, j)),
                  pl.BlockSpec((None, d, tf), lambda i, j: (layer, 0, j)),
                  pl.BlockSpec((FFN_CONV, tf), lambda i, j: (0, j))],
        out_specs=pl.BlockSpec((tm, tf), lambda i, j: (i, j)),
        out_shape=jax.ShapeDtypeStruct((m, f), bf16),
        scratch_shapes=[pltpu.VMEM((tm + SUBLANES, tf), f32),
                        pltpu.VMEM((f // tf, SUBLANES, tf), f32)],
        compiler_params=pltpu.CompilerParams(
            dimension_semantics=("arbitrary", "arbitrary"), vmem_limit_bytes=VMEM_LIMIT_BYTES),
        name="ffn_gate_up",
    )(h, wg, wu, cw)


def _ssd_kernel(xbc_ref, z_ref, nar_ref, cw_ref, cb_ref, dtb_ref, alog_ref, dsk_ref, nw_ref, exp_ref,
                o_ref, buf_ref, state_ref):
    q = z_ref.shape[0]
    hpg = SSD_HEADS // SSD_GROUPS
    gw = MIX_W // SSD_GROUPS

    @pl.when(pl.program_id(1) == 0)
    def _():
        buf_ref[0:SUBLANES, :] = jnp.zeros((SUBLANES, SSD_XBC), f32)
        state_ref[...] = jnp.zeros_like(state_ref)

    xc = _silu(_causal_conv(xbc_ref[...], buf_ref, cw_ref[...], SSD_CONV) + cb_ref[...])
    x = xc[:, :MIX_W]
    dt = _softplus(nar_ref[...] + dtb_ref[...])
    d_a = dt * (-jnp.exp(alog_ref[...]))
    a_cs = _sel_dot_l(_tri_ones(q), d_a)
    a_cs_t = a_cs.T
    a_end = a_cs[q - 1:q, :]
    expand = exp_ref[...]
    dt_e = _sel_dot_r(dt, expand)
    ea_e = _sel_dot_r(jnp.exp(a_cs), expand)
    de_e = _sel_dot_r(jnp.exp(a_end - a_cs), expand)
    xdt = x * dt_e
    xde = (xdt * de_e).astype(bf16)
    causal = _iota2((q, q), 0) >= _iota2((q, q), 1)
    lane = _iota2((1, LANES), 1)
    y_groups = []
    for g in range(SSD_GROUPS):
        bm = xc[:, MIX_W + g * SSD_STATE:MIX_W + (g + 1) * SSD_STATE].astype(bf16)
        cm = xc[:, MIX_W + (SSD_GROUPS + g) * SSD_STATE:MIX_W + (SSD_GROUPS + g + 1) * SSD_STATE].astype(bf16)
        scores = _dot_nt(cm, bm)
        pieces = []
        for pair in range(hpg // 2):
            c0 = g * gw + pair * LANES
            xp = xdt[:, c0:c0 + LANES]
            acc = None
            for half in range(2):
                h = g * hpg + pair * 2 + half
                seg = a_cs[:, h:h + 1] - a_cs_t[h:h + 1, :]
                p = (scores * jnp.exp(jnp.where(causal, seg, -jnp.inf))).astype(bf16)
                in_half = (lane >= half * SSD_HEAD_DIM) & (lane < (half + 1) * SSD_HEAD_DIM)
                term = _dot(p, jnp.where(in_half, xp, 0.0).astype(bf16))
                acc = term if acc is None else acc + term
            pieces.append(acc)
        y_intra = jnp.concatenate(pieces, axis=1)
        s_prev = state_ref[g]
        y_inter = _dot(cm, s_prev.astype(bf16)) * ea_e[:, g * gw:(g + 1) * gw]
        state_ref[g] = s_prev * ea_e[q - 1:q, g * gw:(g + 1) * gw] + _dot_tn(bm, xde[:, g * gw:(g + 1) * gw])
        y_groups.append(y_intra + y_inter)
    y = jnp.concatenate(y_groups, axis=1) + x * dsk_ref[...]
    y = y * _silu(z_ref[...])
    nw = nw_ref[...]
    out = [_group_rms(y[:, g * gw:(g + 1) * gw], nw[:, g * gw:(g + 1) * gw]) for g in range(SSD_GROUPS)]
    o_ref[...] = jnp.concatenate(out, axis=1).astype(o_ref.dtype)


def _ssd_mixer(proj, bsz, seq, conv_w, conv_b, dt_bias, a_log, d_skip, norm_w):
    q = SSD_Q
    assert seq % q == 0
    nc = seq // q
    head_of_col = jnp.arange(MIX_W) // SSD_HEAD_DIM
    expand = (jnp.arange(LANES)[:, None] == head_of_col[None, :]).astype(bf16)
    row = lambda i, c: (i * nc + c, 0)
    const = lambda i, c: (0, 0)
    return pl.pallas_call(
        _ssd_kernel,
        grid=(bsz, nc),
        in_specs=[pl.BlockSpec((q, SSD_XBC), lambda i, c: (i * nc + c, COL_SSD_XBC // SSD_XBC)),
                  pl.BlockSpec((q, MIX_W), lambda i, c: (i * nc + c, COL_SSD_Z // MIX_W)),
                  pl.BlockSpec((q, LANES), lambda i, c: (i * nc + c, COL_NARROW // LANES)),
                  pl.BlockSpec((SSD_CONV, SSD_XBC), const),
                  pl.BlockSpec((1, SSD_XBC), const),
                  pl.BlockSpec((1, LANES), const),
                  pl.BlockSpec((1, LANES), const),
                  pl.BlockSpec((1, MIX_W), const),
                  pl.BlockSpec((1, MIX_W), const),
                  pl.BlockSpec((LANES, MIX_W), const)],
        out_specs=pl.BlockSpec((q, MIX_W), row),
        out_shape=jax.ShapeDtypeStruct((bsz * seq, MIX_W), bf16),
        scratch_shapes=[pltpu.VMEM((q + SUBLANES, SSD_XBC), f32),
                        pltpu.VMEM((SSD_GROUPS, SSD_STATE, MIX_W // SSD_GROUPS), f32)],
        compiler_params=pltpu.CompilerParams(
            dimension_semantics=("parallel", "arbitrary"), vmem_limit_bytes=VMEM_LIMIT_BYTES),
        name="ssd_mixer",
    )(proj, proj, proj, conv_w.astype(f32), conv_b.reshape(1, SSD_XBC).astype(f32),
      _pad_lanes(dt_bias, LANE_SSD_DT), _pad_lanes(a_log, LANE_SSD_DT),
      jnp.repeat(d_skip.astype(f32), SSD_HEAD_DIM).reshape(1, MIX_W), norm_w.reshape(1, MIX_W).astype(f32), expand)


def _s5_tables(lam_re, lam_im, b_re, b_im, c_re, c_im, log_dt, scan_len):
    hp = lax.Precision.HIGHEST
    t, gb, nb, nh, ns = S5_T, S5_GB, S5_GROUPS // S5_GB, S5_GROUP_CH, S5_STATE
    dt = jnp.exp(log_dt.astype(f32))[:, None]
    ar, ai = lam_re.astype(f32) * dt, lam_im.astype(f32) * dt
    mag = jnp.exp(ar)
    lb_re, lb_im = mag * jnp.cos(ai), mag * jnp.sin(ai)
    den = lam_re * lam_re + lam_im * lam_im
    f_re = ((lb_re - 1.0) * lam_re + lb_im * lam_im) / den
    f_im = (lb_im * lam_re - (lb_re - 1.0) * lam_im) / den
    bb_re = f_re[..., None] * b_re - f_im[..., None] * b_im
    bb_im = f_re[..., None] * b_im + f_im[..., None] * b_re
    d = jnp.arange(t + 1, dtype=f32)[:, None, None]
    pw_mag = jnp.exp(d * ar)
    pw_re, pw_im = pw_mag * jnp.cos(d * ai), pw_mag * jnp.sin(d * ai)
    cr, ci = c_re.astype(f32), c_im.astype(f32)
    m_re = cr[None] * pw_re[:, :, None, :] - ci[None] * pw_im[:, :, None, :]
    m_im = cr[None] * pw_im[:, :, None, :] + ci[None] * pw_re[:, :, None, :]
    kern = (jnp.einsum('dgop,gpi->dgoi', m_re[:t], bb_re, precision=hp)
            - jnp.einsum('dgop,gpi->dgoi', m_im[:t], bb_im, precision=hp))
    eye = jnp.eye(gb, dtype=f32)

    def block_diag(x):
        shape_eye = [1] * (x.ndim + 1)
        shape_eye[2] = gb
        shape_eye[-2] = gb
        return x[..., None, :] * eye.reshape(shape_eye)

    lagk = kern.reshape(t, nb, gb, nh, nh).transpose(1, 0, 2, 4, 3)
    lagk = block_diag(lagk).reshape(nb, t, gb * nh, gb * nh)
    d_rev = jnp.arange(t - 1, -1, -1).astype(f32)[:, None, None]
    rev_mag = jnp.exp(d_rev * ar)
    rev_re, rev_im = rev_mag * jnp.cos(d_rev * ai), rev_mag * jnp.sin(d_rev * ai)
    v_re = rev_re[..., None] * bb_re[None] - rev_im[..., None] * bb_im[None]
    v_im = rev_re[..., None] * bb_im[None] + rev_im[..., None] * bb_re[None]
    vmat = jnp.stack([v_re, v_im], axis=0).reshape(2, t, nb, gb, ns, nh)
    vmat = vmat.transpose(2, 1, 3, 5, 0, 4)
    vmat = block_diag(vmat).reshape(nb, t * gb * nh, 2 * gb * ns)
    wmat = jnp.stack([m_re[1:], -m_im[1:]], axis=0).reshape(2, t, nb, gb, nh, ns)
    wmat = wmat.transpose(2, 0, 3, 5, 1, 4)
    wmat = block_diag(wmat).reshape(nb, 2 * gb * ns, t * gb * nh)
    a_re, a_im = pw_re[t].reshape(nb, gb * ns), pw_im[t].reshape(nb, gb * ns)
    c1, c2 = [], []
    for _ in range(max(1, (scan_len - 1).bit_length())):
        c1.append(jnp.concatenate([a_re, a_re], axis=-1))
        c2.append(jnp.concatenate([-a_im, a_im], axis=-1))
        a_re, a_im = a_re * a_re - a_im * a_im, 2.0 * a_re * a_im
    return lagk.astype(bf16), vmat.astype(bf16), wmat.astype(bf16), jnp.stack(c1, axis=1), jnp.stack(c2, axis=1)


def _s5_scan_kernel(u_ref, lagk_ref, v_ref, w_ref, c1_ref, c2_ref, o_ref, toep_ref):
    t, cps = u_ref.shape[0], u_ref.shape[1]
    half = c1_ref.shape[1] // 2

    @pl.when(pl.program_id(1) == 0)
    def _():
        for j in range(t):
            for tt in range(t):
                blk = lagk_ref[tt - j] if tt >= j else jnp.zeros((LANES, LANES), bf16)
                toep_ref[j * LANES:(j + 1) * LANES, tt * LANES:(tt + 1) * LANES] = blk

    x_all = jnp.concatenate([u_ref[j].astype(bf16) for j in range(t)], axis=1)
    z = _dot(x_all, v_ref[...])
    pos = _iota2((cps, 1), 0)
    c1, c2 = c1_ref[...], c2_ref[...]
    s = z
    for k in range(c1.shape[0]):
        step = 1 << k
        if step >= cps:
            break
        sh = jnp.where(pos >= step, pltpu.roll(s, step, 0), 0.0)
        s = s + sh * c1[k:k + 1, :] + pltpu.roll(sh, half, 1) * c2[k:k + 1, :]
    s_in = jnp.where(pos >= 1, pltpu.roll(s, 1, 0), 0.0)
    y = _dot(x_all, toep_ref[...]) + _dot(s_in.astype(bf16), w_ref[...])
    for j in range(t):
        o_ref[j] = y[:, j * LANES:(j + 1) * LANES]


def _s5_glu_kernel(y_ref, u_ref, d_ref, gw_ref, gb_ref, o_ref):
    y = y_ref[...] + d_ref[...] * u_ref[...]
    y = 0.5 * y * (1.0 + jnp.tanh(math.sqrt(2.0 / math.pi) * (y + 0.044715 * (y * y * y))))
    gate = _dot(y.astype(bf16), gw_ref[...]) + gb_ref[...]
    o_ref[...] = (y * jax.nn.sigmoid(gate)).astype(o_ref.dtype)


def _chunk_major(a, t):
    rows, w = a.shape
    return a.reshape(rows // t, t, w).transpose(1, 0, 2)


def _token_major(a3):
    t, chunks, w = a3.shape
    return a3.transpose(1, 0, 2).reshape(chunks * t, w)


def _s5_mixer(u3, bsz, seq, lam_re, lam_im, b_re, b_im, c_re, c_im, d_skip, log_dt, glu_w, glu_b, tm=512):
    t = S5_T
    assert seq % t == 0 and u3.shape == (t, bsz * seq // t, MIX_W)
    m = bsz * seq
    cps = seq // t
    nb = S5_GROUPS // S5_GB
    lagk, vmat, wmat, c1, c2 = _s5_tables(lam_re, lam_im, b_re, b_im, c_re, c_im, log_dt, cps)
    nk = c1.shape[1]
    tw, sw = t * LANES, 2 * S5_GB * S5_STATE
    per_block = lambda blk, b: (blk, 0, 0)
    seq_block = pl.BlockSpec((t, cps, LANES), lambda blk, b: (0, b, blk))
    y3 = pl.pallas_call(
        _s5_scan_kernel,
        grid=(nb, bsz),
        in_specs=[seq_block,
                  pl.BlockSpec((None, t, LANES, LANES), lambda blk, b: (blk, 0, 0, 0)),
                  pl.BlockSpec((None, tw, sw), per_block),
                  pl.BlockSpec((None, sw, tw), per_block),
                  pl.BlockSpec((None, nk, sw), per_block),
                  pl.BlockSpec((None, nk, sw), per_block)],
        out_specs=seq_block,
        out_shape=jax.ShapeDtypeStruct(u3.shape, f32),
        scratch_shapes=[pltpu.VMEM((tw, tw), bf16)],
        compiler_params=pltpu.CompilerParams(
            dimension_semantics=("arbitrary", "arbitrary"), vmem_limit_bytes=VMEM_LIMIT_BYTES),
        name="s5_scan",
    )(u3, lagk, vmat, wmat, c1, c2)
    assert m % tm == 0
    row_block = pl.BlockSpec((tm, MIX_W), lambda i: (i, 0))
    const = lambda i: (0, 0)
    out = pl.pallas_call(
        _s5_glu_kernel,
        grid=(m // tm,),
        in_specs=[row_block, row_block,
                  pl.BlockSpec((1, MIX_W), const),
                  pl.BlockSpec((MIX_W, MIX_W), const),
                  pl.BlockSpec((1, MIX_W), const)],
        out_specs=row_block,
        out_shape=jax.ShapeDtypeStruct((m, MIX_W), bf16),
        compiler_params=pltpu.CompilerParams(dimension_semantics=("parallel",)),
        name="s5_glu",
    )(y3.reshape(m, MIX_W), u3.reshape(m, MIX_W), d_skip.reshape(1, MIX_W).astype(f32), glu_w.astype(bf16),
      glu_b.reshape(1, MIX_W).astype(f32))
    return _token_major(out.reshape(t, m // t, MIX_W))


def _gla_kernel(qk_ref, v_ref, r_ref, nar_ref, w2_ref, gb_ref, nw_ref, sel_ref, o_ref, state_ref):
    q = v_ref.shape[0]
    levels = q.bit_length() - 1
    dkw = GLA_HEADS * GLA_DK

    @pl.when(pl.program_id(1) == 0)
    def _():
        state_ref[...] = jnp.zeros_like(state_ref)

    gate = _dot(nar_ref[...].astype(bf16), w2_ref[...]) + gb_ref[...]
    log_a = -_softplus(-gate) * (1.0 / GLA_GATE_NORM)
    b = _sel_dot_l(_tri_ones(q), log_a)
    refs = _sel_dot_l(sel_ref[...], b)
    qk = qk_ref[...]
    qs = qk[:, :dkw] * (GLA_DK ** -0.5)
    ks = qk[:, dkw:]
    v = v_ref[...]
    rgate = _silu(r_ref[...])
    nw = nw_ref[...]
    row = _iota2((q, 1), 0)
    ri, ci = _iota2((q, q), 0), _iota2((q, q), 1)
    eb = jnp.exp(b)
    b_end = b[q - 1:q, :]
    k_end = (ks * jnp.exp(b_end - b)).astype(bf16)
    out = []
    for h in range(GLA_HEADS):
        sl = slice(h * GLA_DK, (h + 1) * GLA_DK)
        qh, kh, bh = qs[:, sl], ks[:, sl], b[:, sl]
        vh = v[:, h * GLA_DV:(h + 1) * GLA_DV].astype(bf16)
        scores = jnp.where(ri == ci, jnp.sum(qh * kh, axis=-1, keepdims=True), 0.0)
        for lv in range(levels):
            s = 1 << lv
            upper = ((row >> lv) & 1) == 1
            rl = refs[lv * q:(lv + 1) * q, sl]
            qt = (qh * jnp.exp(jnp.where(upper, bh - rl, -jnp.inf))).astype(bf16)
            kt = (kh * jnp.exp(jnp.where(upper, -jnp.inf, rl - bh))).astype(bf16)
            same_block = (ri >> (lv + 1)) == (ci >> (lv + 1))
            scores = scores + jnp.where(same_block, _dot_nt(qt, kt), 0.0)
        s_prev = state_ref[h]
        o = _dot(scores.astype(bf16), vh) + _dot_nt((qh * eb[:, sl]).astype(bf16), s_prev.astype(bf16))
        state_ref[h] = s_prev * jnp.exp(b_end[:, sl]) + _dot_tn(vh, k_end[:, sl])
        o = _group_rms(o, nw) * rgate[:, h * GLA_DV:(h + 1) * GLA_DV]
        out.append(o)
    o_ref[...] = jnp.concatenate(out, axis=1).astype(o_ref.dtype)


def _gla_mixer(proj, bsz, seq, gate_w2, gate_b, norm_w):
    q = GLA_Q
    assert seq % q == 0 and q & (q - 1) == 0
    nc = seq // q
    levels = q.bit_length() - 1
    i = jnp.arange(q)
    sel = jnp.concatenate(
        [(((i // (2 << lv)) * (2 << lv) + (1 << lv))[:, None] == i[None, :]) for lv in range(levels)], axis=0).astype(bf16)
    w2 = jnp.zeros((LANES, GLA_HEADS * GLA_DK), f32).at[LANE_GLA_G:LANE_GLA_G + GLA_GATE_RANK].set(gate_w2.astype(f32))
    const = lambda i, c: (0, 0)
    blk = lambda col: pl.BlockSpec((q, MIX_W), lambda i, c: (i * nc + c, col // MIX_W))
    return pl.pallas_call(
        _gla_kernel,
        grid=(bsz, nc),
        in_specs=[blk(COL_GLA_QK), blk(COL_GLA_V), blk(COL_GLA_R),
                  pl.BlockSpec((q, LANES), lambda i, c: (i * nc + c, COL_NARROW // LANES)),
                  pl.BlockSpec((LANES, GLA_HEADS * GLA_DK), const),
                  pl.BlockSpec((1, GLA_HEADS * GLA_DK), const),
                  pl.BlockSpec((1, GLA_DV), const),
                  pl.BlockSpec((levels * q, q), const)],
        out_specs=pl.BlockSpec((q, MIX_W), lambda i, c: (i * nc + c, 0)),
        out_shape=jax.ShapeDtypeStruct((bsz * seq, MIX_W), bf16),
        scratch_shapes=[pltpu.VMEM((GLA_HEADS, GLA_DV, GLA_DK), f32)],
        compiler_params=pltpu.CompilerParams(
            dimension_semantics=("parallel", "arbitrary"), vmem_limit_bytes=VMEM_LIMIT_BYTES),
        name="gla_mixer",
    )(proj, proj, proj, proj, w2.astype(bf16), gate_b.reshape(1, -1).astype(f32),
      norm_w.reshape(1, GLA_DV).astype(f32), sel)


def _gdn_kernel(qkv_ref, z_ref, nar_ref, cw_ref, alog_ref, dtb_ref, nw_ref, o_ref, buf_ref, state_ref):
    c = z_ref.shape[0]
    hw = GDN_HEADS * GDN_DK

    @pl.when(pl.program_id(1) == 0)
    def _():
        buf_ref[0:SUBLANES, :] = jnp.zeros((SUBLANES, 3 * MIX_W), f32)
        state_ref[...] = jnp.zeros_like(state_ref)

    qkv = _silu(_causal_conv(qkv_ref[...], buf_ref, cw_ref[...], GDN_CONV))
    nar = nar_ref[...]
    beta = jax.nn.sigmoid(nar)
    g = -jnp.exp(alog_ref[...]) * _softplus(nar + dtb_ref[...])
    gc = _sel_dot_l(_tri_ones(c), g)
    gc_t = gc.T
    eg = jnp.exp(gc)
    eg_rev = jnp.exp(gc[c - 1:c, :] - gc)
    ri, ci = _iota2((c, c), 0), _iota2((c, c), 1)
    z = z_ref[...]
    nw = nw_ref[...]
    heads = range(GDN_HEADS)
    qn, kn, kb16, kh16, decay, egc, x_cur = [], [], [], [], [], [], []
    for h in heads:
        qh = qkv[:, h * GDN_DK:(h + 1) * GDN_DK]
        kh = qkv[:, hw + h * GDN_DK:hw + (h + 1) * GDN_DK]
        vh = qkv[:, 2 * hw + h * GDN_DV:2 * hw + (h + 1) * GDN_DV]
        qn.append(qh * lax.rsqrt(jnp.sum(qh * qh, axis=-1, keepdims=True) + 1e-6) * (GDN_DK ** -0.5))
        kh = kh * lax.rsqrt(jnp.sum(kh * kh, axis=-1, keepdims=True) + 1e-6)
        kn.append(kh)
        bh = beta[:, LANE_GDN_B + h:LANE_GDN_B + h + 1]
        la = LANE_GDN_A + h
        egc.append(eg[:, la:la + 1])
        decay.append(jnp.exp(jnp.where(ri >= ci, gc[:, la:la + 1] - gc_t[la:la + 1, :], -jnp.inf)))
        kb = kh * bh
        kb16.append(kb.astype(bf16))
        kh16.append(kh.astype(bf16))
        x_cur.append(jnp.concatenate([vh * bh, kb * egc[h]], axis=1))
    p_cur = [-jnp.where(ri > ci, _dot_nt(kb16[h], kh16[h]) * decay[h], 0.0) for h in heads]
    n_fac = c.bit_length() - 1
    for k in range(n_fac):
        p_hi = [p.astype(bf16) for p in p_cur]
        px = [_dot(p_hi[h], x_cur[h].astype(bf16)) for h in heads]
        x_cur = [x_cur[h] + px[h] for h in heads]
        if k + 1 < n_fac:
            p_lo = [(p - ph.astype(f32)).astype(bf16) for p, ph in zip(p_cur, p_hi)]
            p_cur = [_dot(p_hi[h], p_hi[h]) + _dot(p_hi[h], p_lo[h]) + _dot(p_lo[h], p_hi[h]) for h in heads]
    s_prev = [state_ref[h] for h in heads]
    s16 = [s.astype(bf16) for s in s_prev]
    v_new = [(x_cur[h][:, :GDN_DV] - _dot(x_cur[h][:, GDN_DV:].astype(bf16), s16[h])).astype(bf16) for h in heads]
    attn = [(_dot_nt(qn[h].astype(bf16), kh16[h]) * decay[h]).astype(bf16) for h in heads]
    o = [_dot((qn[h] * egc[h]).astype(bf16), s16[h]) + _dot(attn[h], v_new[h]) for h in heads]
    for h in heads:
        la = LANE_GDN_A + h
        k_end = (kn[h] * eg_rev[:, la:la + 1]).astype(bf16)
        state_ref[h] = s_prev[h] * eg[c - 1:c, la:la + 1] + _dot_tn(k_end, v_new[h])
    out = [_group_rms(o[h], nw) * _silu(z[:, h * GDN_DV:(h + 1) * GDN_DV]) for h in heads]
    o_ref[...] = jnp.concatenate(out, axis=1).astype(o_ref.dtype)


def _gdn_mixer(proj, bsz, seq, conv_w, a_log, dt_bias, norm_w):
    c = GDN_C
    assert seq % c == 0 and c & (c - 1) == 0
    nc = seq // c
    const = lambda i, cc: (0, 0)
    return pl.pallas_call(
        _gdn_kernel,
        grid=(bsz, nc),
        in_specs=[pl.BlockSpec((c, 3 * MIX_W), lambda i, cc: (i * nc + cc, COL_GDN_QKV // (3 * MIX_W))),
                  pl.BlockSpec((c, MIX_W), lambda i, cc: (i * nc + cc, COL_GDN_Z // MIX_W)),
                  pl.BlockSpec((c, LANES), lambda i, cc: (i * nc + cc, COL_NARROW // LANES)),
                  pl.BlockSpec((GDN_CONV, 3 * MIX_W), const),
                  pl.BlockSpec((1, LANES), const),
                  pl.BlockSpec((1, LANES), const),
                  pl.BlockSpec((1, GDN_DV), const)],
        out_specs=pl.BlockSpec((c, MIX_W), lambda i, cc: (i * nc + cc, 0)),
        out_shape=jax.ShapeDtypeStruct((bsz * seq, MIX_W), bf16),
        scratch_shapes=[pltpu.VMEM((c + SUBLANES, 3 * MIX_W), f32),
                        pltpu.VMEM((GDN_HEADS, GDN_DK, GDN_DV), f32)],
        compiler_params=pltpu.CompilerParams(
            dimension_semantics=("parallel", "arbitrary"), vmem_limit_bytes=VMEM_LIMIT_BYTES),
        name="gdn_mixer",
    )(proj, proj, proj, conv_w.astype(f32), _pad_lanes(a_log, LANE_GDN_A), _pad_lanes(dt_bias, LANE_GDN_A),
      norm_w.reshape(1, GDN_DV).astype(f32))


def kernel(x, norm_mix_pre, norm_mix_post, norm_ffn_pre, norm_ffn_post, w_in, w_out, ssd_conv_w, ssd_conv_b, ssd_dt_bias, ssd_a_log, ssd_d, ssd_norm, s5_lambda_re, s5_lambda_im, s5_b_re, s5_b_im, s5_c_re, s5_c_im, s5_d, s5_log_dt, s5_glu_w, s5_glu_b, gla_gate_w2, gla_gate_b, gla_norm, gdn_conv_w, gdn_a_log, gdn_dt_bias, gdn_norm, ffn_w_gate, ffn_w_up, ffn_conv_w, ffn_w_down):
    bsz, seq, d = x.shape
    m = bsz * seq
    xf = x.reshape(m, d)
    h = _prenorm(xf, norm_mix_pre[0])
    for l in range(DEPTH):
        proj = _matmul(h, _permute_w_in(w_in[l]), f32, 1024, PROJ_TN)
        y_ssd = _ssd_mixer(proj, bsz, seq, ssd_conv_w[l], ssd_conv_b[l], ssd_dt_bias[l], ssd_a_log[l],
                           ssd_d[l], ssd_norm[l])
        h_cm = _chunk_major(h, S5_T).reshape(m, d)
        u3 = _matmul(h_cm, _s5_w_in(w_in[l]), f32, 1024, 1024).reshape(S5_T, m // S5_T, MIX_W)
        y_s5 = _s5_mixer(u3, bsz, seq, s5_lambda_re[l], s5_lambda_im[l], s5_b_re[l], s5_b_im[l],
                         s5_c_re[l], s5_c_im[l], s5_d[l], s5_log_dt[l], s5_glu_w[l], s5_glu_b[l])
        y_gla = _gla_mixer(proj, bsz, seq, gla_gate_w2[l], gla_gate_b[l], gla_norm[l])
        y_gdn = _gdn_mixer(proj, bsz, seq, gdn_conv_w[l], gdn_a_log[l], gdn_dt_bias[l], gdn_norm[l])
        mixed = _outproj([y_ssd, y_s5, y_gla, y_gdn], w_out[l].astype(bf16))
        xf, h = _postnorm_residual_prenorm(xf, mixed, norm_mix_post[l], norm_ffn_pre[l])
        act = _ffn_gate_up(h, ffn_w_gate, ffn_w_up, ffn_conv_w[l], l, seq)
        down = _matmul_ksplit(act, ffn_w_down[l].astype(bf16), f32, 512, 1024, D_FF // 2)
        if l + 1 < DEPTH:
            xf, h = _postnorm_residual_prenorm(xf, down, norm_ffn_post[l], norm_mix_pre[l + 1])
        else:
            xf = _postnorm_residual(xf, down, norm_ffn_post[l])
    return xf.reshape(bsz, seq, d)
```

```python
import functools
import math

import jax
import jax.numpy as jnp
from jax import lax
from jax.experimental import pallas as pl
from jax.experimental.pallas import tpu as pltpu

D_MODEL = 4096
DEPTH = 2
MIX_W = D_MODEL // 4
RMS_EPS = 1e-6
D_FF = 11008
FFN_CONV = 3

SSD_HEAD_DIM = 64
SSD_HEADS = MIX_W // SSD_HEAD_DIM
SSD_STATE = 128
SSD_GROUPS = 2
SSD_CONV = 4
SSD_XBC = MIX_W + 2 * SSD_GROUPS * SSD_STATE
SSD_Q = 128

S5_GROUP_CH = 16
S5_GROUPS = MIX_W // S5_GROUP_CH
S5_STATE = 64
S5_T = 16
S5_GB = 8

GLA_HEADS = 4
GLA_DK = MIX_W // (2 * GLA_HEADS)
GLA_DV = MIX_W // GLA_HEADS
GLA_GATE_RANK = 16
GLA_GATE_NORM = 16.0
GLA_Q = 128

GDN_HEADS = 8
GDN_DK = MIX_W // GDN_HEADS
GDN_DV = MIX_W // GDN_HEADS
GDN_CONV = 4
GDN_C = 64

SPLIT_SIZES = (
    MIX_W, SSD_XBC, SSD_HEADS,
    MIX_W,
    GLA_HEADS * GLA_DK, GLA_HEADS * GLA_DK, MIX_W, MIX_W, GLA_GATE_RANK,
    GDN_HEADS * GDN_DK, GDN_HEADS * GDN_DK, GDN_HEADS * GDN_DV, MIX_W, GDN_HEADS, GDN_HEADS,
)
D_IN_PROJ = sum(SPLIT_SIZES)
(SEG_SSD_Z, SEG_SSD_XBC, SEG_SSD_DT, SEG_S5_U, SEG_GLA_Q, SEG_GLA_K, SEG_GLA_V, SEG_GLA_R, SEG_GLA_G,
 SEG_GDN_Q, SEG_GDN_K, SEG_GDN_V, SEG_GDN_Z, SEG_GDN_B, SEG_GDN_A) = range(len(SPLIT_SIZES))

LANES = 128
SUBLANES = 8
VMEM_LIMIT_BYTES = 56 * 1024 * 1024

N_PROJ_PAD = 10240
PROJ_TN = 1024
COL_GDN_QKV = 0
COL_SSD_XBC = 3072
COL_NARROW = 4608
COL_GLA_QK = 5120
COL_GLA_V = 6144
COL_GLA_R = 7168
COL_GDN_Z = 8192
COL_SSD_Z = 9216
LANE_SSD_DT = 0
LANE_GLA_G = 16
LANE_GDN_B = 32
LANE_GDN_A = 40
_DST_OFF = {
    SEG_GDN_Q: COL_GDN_QKV, SEG_GDN_K: COL_GDN_QKV + 1024, SEG_GDN_V: COL_GDN_QKV + 2048,
    SEG_SSD_XBC: COL_SSD_XBC,
    SEG_SSD_DT: COL_NARROW + LANE_SSD_DT, SEG_GLA_G: COL_NARROW + LANE_GLA_G,
    SEG_GDN_B: COL_NARROW + LANE_GDN_B, SEG_GDN_A: COL_NARROW + LANE_GDN_A,
    SEG_GLA_Q: COL_GLA_QK, SEG_GLA_K: COL_GLA_QK + 512, SEG_GLA_V: COL_GLA_V, SEG_GLA_R: COL_GLA_R,
    SEG_GDN_Z: COL_GDN_Z, SEG_SSD_Z: COL_SSD_Z,
}

f32 = jnp.float32
bf16 = jnp.bfloat16


def _permute_w_in(w):
    src = [0]
    for s in SPLIT_SIZES:
        src.append(src[-1] + s)
    order = sorted(_DST_OFF, key=_DST_OFF.get)
    parts, pos = [], 0
    for i in order:
        if _DST_OFF[i] > pos:
            parts.append(jnp.zeros((w.shape[0], _DST_OFF[i] - pos), w.dtype))
        parts.append(w[:, src[i]:src[i + 1]])
        pos = _DST_OFF[i] + SPLIT_SIZES[i]
    if N_PROJ_PAD > pos:
        parts.append(jnp.zeros((w.shape[0], N_PROJ_PAD - pos), w.dtype))
    return jnp.concatenate(parts, axis=1).astype(bf16)


def _s5_w_in(w):
    off = sum(SPLIT_SIZES[:SEG_S5_U])
    return w[:, off:off + MIX_W].astype(bf16)


def _pad_lanes(vec, offset):
    return jnp.zeros((1, LANES), f32).at[0, offset:offset + vec.shape[0]].set(vec.astype(f32))


def _dot(a, b):
    return jnp.dot(a, b, preferred_element_type=f32)


def _dot_nt(a, b):
    return lax.dot_general(a, b, (((1,), (1,)), ((), ())), preferred_element_type=f32)


def _dot_tn(a, b):
    return lax.dot_general(a, b, (((0,), (0,)), ((), ())), preferred_element_type=f32)


def _split3(x):
    hi = x.astype(bf16)
    r1 = x - hi.astype(f32)
    mid = r1.astype(bf16)
    lo = (r1 - mid.astype(f32)).astype(bf16)
    return hi, mid, lo


def _sel_dot_l(sel, x):
    hi, mid, lo = _split3(x)
    return _dot(sel, hi) + _dot(sel, mid) + _dot(sel, lo)


def _sel_dot_r(x, sel):
    hi, mid, lo = _split3(x)
    return _dot(hi, sel) + _dot(mid, sel) + _dot(lo, sel)


def _softplus(x):
    return jnp.maximum(x, 0.0) + jnp.log1p(jnp.exp(-jnp.abs(x)))


def _silu(x):
    return x * jax.nn.sigmoid(x)


def _iota2(shape, dim):
    return lax.broadcasted_iota(jnp.int32, shape, dim)


def _tri_ones(n):
    return (_iota2((n, n), 0) >= _iota2((n, n), 1)).astype(bf16)


def _causal_conv(x, buf_ref, cw, kw):
    q = x.shape[0]
    buf_ref[SUBLANES:SUBLANES + q, :] = x
    acc = cw[kw - 1:kw, :] * x
    for k in range(kw - 1):
        s = SUBLANES - (kw - 1) + k
        acc = acc + cw[k:k + 1, :] * buf_ref[s:s + q, :]
    buf_ref[0:SUBLANES, :] = x[q - SUBLANES:q, :]
    return acc


def _group_rms(y, nw):
    ms = jnp.mean(y * y, axis=-1, keepdims=True)
    return y * lax.rsqrt(ms + RMS_EPS) * nw


def _matmul_kernel(a_ref, w_ref, o_ref):
    o_ref[...] = _dot(a_ref[...], w_ref[...]).astype(o_ref.dtype)


def _matmul(a, w, out_dtype, tm, tn):
    m, k = a.shape
    n = w.shape[1]
    assert m % tm == 0 and n % tn == 0
    return pl.pallas_call(
        _matmul_kernel,
        grid=(m // tm, n // tn),
        in_specs=[pl.BlockSpec((tm, k), lambda i, j: (i, 0)),
                  pl.BlockSpec((k, tn), lambda i, j: (0, j))],
        out_specs=pl.BlockSpec((tm, tn), lambda i, j: (i, j)),
        out_shape=jax.ShapeDtypeStruct((m, n), out_dtype),
        compiler_params=pltpu.CompilerParams(
            dimension_semantics=("parallel", "arbitrary"), vmem_limit_bytes=VMEM_LIMIT_BYTES),
        name="matmul",
    )(a, w)


def _outproj_kernel(a0_ref, a1_ref, a2_ref, a3_ref, w_ref, o_ref):
    acc = _dot(a0_ref[...], w_ref[0:MIX_W, :])
    acc = acc + _dot(a1_ref[...], w_ref[MIX_W:2 * MIX_W, :])
    acc = acc + _dot(a2_ref[...], w_ref[2 * MIX_W:3 * MIX_W, :])
    acc = acc + _dot(a3_ref[...], w_ref[3 * MIX_W:4 * MIX_W, :])
    o_ref[...] = acc


def _outproj(parts, w, tm=1024, tn=1024):
    m = parts[0].shape[0]
    k, n = w.shape
    assert m % tm == 0 and n % tn == 0 and k == 4 * MIX_W
    a_spec = pl.BlockSpec((tm, MIX_W), lambda i, j: (i, 0))
    return pl.pallas_call(
        _outproj_kernel,
        grid=(m // tm, n // tn),
        in_specs=[a_spec, a_spec, a_spec, a_spec, pl.BlockSpec((k, tn), lambda i, j: (0, j))],
        out_specs=pl.BlockSpec((tm, tn), lambda i, j: (i, j)),
        out_shape=jax.ShapeDtypeStruct((m, n), f32),
        compiler_params=pltpu.CompilerParams(
            dimension_semantics=("parallel", "arbitrary"), vmem_limit_bytes=VMEM_LIMIT_BYTES),
        name="outproj",
    )(*parts, w)


def _matmul_ksplit_kernel(a_ref, w_ref, o_ref, acc_ref):
    k = pl.program_id(2)
    part = _dot(a_ref[...], w_ref[...])

    @pl.when(k == 0)
    def _():
        acc_ref[...] = part

    @pl.when(k != 0)
    def _():
        acc_ref[...] += part

    @pl.when(k == pl.num_programs(2) - 1)
    def _():
        o_ref[...] = acc_ref[...].astype(o_ref.dtype)


def _matmul_ksplit(a, w, out_dtype, tm, tn, tk):
    m, k = a.shape
    n = w.shape[1]
    assert m % tm == 0 and n % tn == 0 and k % tk == 0
    return pl.pallas_call(
        _matmul_ksplit_kernel,
        grid=(m // tm, n // tn, k // tk),
        in_specs=[pl.BlockSpec((tm, tk), lambda i, j, kk: (i, kk)),
                  pl.BlockSpec((tk, tn), lambda i, j, kk: (kk, j))],
        out_specs=pl.BlockSpec((tm, tn), lambda i, j, kk: (i, j)),
        out_shape=jax.ShapeDtypeStruct((m, n), out_dtype),
        scratch_shapes=[pltpu.VMEM((tm, tn), f32)],
        compiler_params=pltpu.CompilerParams(
            dimension_semantics=("parallel", "arbitrary", "arbitrary"), vmem_limit_bytes=VMEM_LIMIT_BYTES),
        name="matmul_ksplit",
    )(a, w)


def _prenorm_kernel(x_ref, w_ref, o_ref):
    x = x_ref[...]
    ms = jnp.mean(x * x, axis=-1, keepdims=True)
    o_ref[...] = (x * lax.rsqrt(ms + RMS_EPS) * w_ref[...]).astype(o_ref.dtype)


def _prenorm(x, w, tm=256):
    m, d = x.shape
    assert m % tm == 0
    return pl.pallas_call(
        _prenorm_kernel,
        grid=(m // tm,),
        in_specs=[pl.BlockSpec((tm, d), lambda i: (i, 0)),
                  pl.BlockSpec((1, d), lambda i: (0, 0))],
        out_specs=pl.BlockSpec((tm, d), lambda i: (i, 0)),
        out_shape=jax.ShapeDtypeStruct((m, d), bf16),
        compiler_params=pltpu.CompilerParams(dimension_semantics=("parallel",)),
        name="prenorm",
    )(x, w.reshape(1, d))


def _postnorm_residual_kernel(x_ref, y_ref, w_ref, o_ref):
    y = y_ref[...]
    ms = jnp.mean(y * y, axis=-1, keepdims=True)
    o_ref[...] = x_ref[...] + y * lax.rsqrt(ms + RMS_EPS) * w_ref[...]


def _postnorm_residual_prenorm_kernel(x_ref, y_ref, w_ref, w2_ref, o_ref, h_ref):
    y = y_ref[...]
    x = x_ref[...] + y * lax.rsqrt(jnp.mean(y * y, axis=-1, keepdims=True) + RMS_EPS) * w_ref[...]
    o_ref[...] = x
    h_ref[...] = (x * lax.rsqrt(jnp.mean(x * x, axis=-1, keepdims=True) + RMS_EPS) * w2_ref[...]).astype(h_ref.dtype)


def _postnorm_residual_prenorm(x, y, w_post, w_pre, tm=256):
    m, d = x.shape
    assert m % tm == 0
    row = pl.BlockSpec((tm, d), lambda i: (i, 0))
    vec = pl.BlockSpec((1, d), lambda i: (0, 0))
    return pl.pallas_call(
        _postnorm_residual_prenorm_kernel,
        grid=(m // tm,),
        in_specs=[row, row, vec, vec],
        out_specs=[row, row],
        out_shape=[jax.ShapeDtypeStruct((m, d), f32), jax.ShapeDtypeStruct((m, d), bf16)],
        compiler_params=pltpu.CompilerParams(dimension_semantics=("parallel",)),
        name="postnorm_residual_prenorm",
    )(x, y, w_post.reshape(1, d), w_pre.reshape(1, d))


def _postnorm_residual(x, y, w, tm=256):
    m, d = x.shape
    assert m % tm == 0
    return pl.pallas_call(
        _postnorm_residual_kernel,
        grid=(m // tm,),
        in_specs=[pl.BlockSpec((tm, d), lambda i: (i, 0)),
                  pl.BlockSpec((tm, d), lambda i: (i, 0)),
                  pl.BlockSpec((1, d), lambda i: (0, 0))],
        out_specs=pl.BlockSpec((tm, d), lambda i: (i, 0)),
        out_shape=jax.ShapeDtypeStruct((m, d), f32),
        compiler_params=pltpu.CompilerParams(dimension_semantics=("parallel",)),
        name="postnorm_residual",
    )(x, y, w.reshape(1, d))


def _ffn_gate_up_kernel(blocks_per_seq, h_ref, wg_ref, wu_ref, cw_ref, o_ref, g_ref, tail_ref):
    i = pl.program_id(0)
    j = pl.program_id(1)
    tm = h_ref.shape[0]

    @pl.when(i % blocks_per_seq == 0)
    def _():
        tail_ref[j] = jnp.zeros(tail_ref.shape[1:], f32)

    h = h_ref[...]
    g = _dot(h, wg_ref[...].astype(bf16))
    g_ref[0:SUBLANES, :] = tail_ref[j]
    g_ref[SUBLANES:SUBLANES + tm, :] = g
    tail_ref[j] = g[tm - SUBLANES:tm, :]
    cw = cw_ref[...]
    conv = (cw[2:3, :] * g
            + cw[1:2, :] * g_ref[SUBLANES - 1:SUBLANES - 1 + tm, :]
            + cw[0:1, :] * g_ref[SUBLANES - 2:SUBLANES - 2 + tm, :])
    u = _dot(h, wu_ref[...].astype(bf16))
    o_ref[...] = (_silu(conv) * u).astype(o_ref.dtype)


def _ffn_gate_up(h, wg, wu, cw, layer, seq, tm=1024, tf=256):
    m, d = h.shape
    f = wg.shape[2]
    assert m % tm == 0 and f % tf == 0 and seq % tm == 0
    return pl.pallas_call(
        functools.partial(_ffn_gate_up_kernel, seq // tm),
        grid=(m // tm, f // tf),
        in_specs=[pl.BlockSpec((tm, d), lambda i, j: (i, 0)),
                  pl.BlockSpec((None, d, tf), lambda i, j: (layer, 0, j)),
                  pl.BlockSpec((None, d, tf), lambda i, j: (layer, 0, j)),
                  pl.BlockSpec((FFN_CONV, tf), lambda i, j: (0, j))],
        out_specs=pl.BlockSpec((tm, tf), lambda i, j: (i, j)),
        out_shape=jax.ShapeDtypeStruct((m, f), bf16),
        scratch_shapes=[pltpu.VMEM((tm + SUBLANES, tf), f32),
                        pltpu.VMEM((f // tf, SUBLANES, tf), f32)],
        compiler_params=pltpu.CompilerParams(
            dimension_semantics=("arbitrary", "arbitrary"), vmem_limit_bytes=VMEM_LIMIT_BYTES),
        name="ffn_gate_up",
    )(h, wg, wu, cw)


def _ssd_kernel(xbc_ref, z_ref, nar_ref, cw_ref, cb_ref, dtb_ref, alog_ref, dsk_ref, nw_ref, exp_ref,
                o_ref, buf_ref, state_ref):
    q = z_ref.shape[0]
    hpg = SSD_HEADS // SSD_GROUPS
    gw = MIX_W // SSD_GROUPS

    @pl.when(pl.program_id(1) == 0)
    def _():
        buf_ref[0:SUBLANES, :] = jnp.zeros((SUBLANES, SSD_XBC), f32)
        state_ref[...] = jnp.zeros_like(state_ref)

    xc = _silu(_causal_conv(xbc_ref[...], buf_ref, cw_ref[...], SSD_CONV) + cb_ref[...])
    x = xc[:, :MIX_W]
    dt = _softplus(nar_ref[...] + dtb_ref[...])
    d_a = dt * (-jnp.exp(alog_ref[...]))
    a_cs = _sel_dot_l(_tri_ones(q), d_a)
    a_cs_t = a_cs.T
    a_end = a_cs[q - 1:q, :]
    expand = exp_ref[...]
    dt_e = _sel_dot_r(dt, expand)
    ea_e = _sel_dot_r(jnp.exp(a_cs), expand)
    de_e = _sel_dot_r(jnp.exp(a_end - a_cs), expand)
    xdt = x * dt_e
    xde = (xdt * de_e).astype(bf16)
    causal = _iota2((q, q), 0) >= _iota2((q, q), 1)
    lane = _iota2((1, LANES), 1)
    y_groups = []
    for g in range(SSD_GROUPS):
        bm = xc[:, MIX_W + g * SSD_STATE:MIX_W + (g + 1) * SSD_STATE].astype(bf16)
        cm = xc[:, MIX_W + (SSD_GROUPS + g) * SSD_STATE:MIX_W + (SSD_GROUPS + g + 1) * SSD_STATE].astype(bf16)
        scores = _dot_nt(cm, bm)
        pieces = []
        for pair in range(hpg // 2):
            c0 = g * gw + pair * LANES
            xp = xdt[:, c0:c0 + LANES]
            acc = None
            for half in range(2):
                h = g * hpg + pair * 2 + half
                seg = a_cs[:, h:h + 1] - a_cs_t[h:h + 1, :]
                p = (scores * jnp.exp(jnp.where(causal, seg, -jnp.inf))).astype(bf16)
                in_half = (lane >= half * SSD_HEAD_DIM) & (lane < (half + 1) * SSD_HEAD_DIM)
                term = _dot(p, jnp.where(in_half, xp, 0.0).astype(bf16))
                acc = term if acc is None else acc + term
            pieces.append(acc)
        y_intra = jnp.concatenate(pieces, axis=1)
        s_prev = state_ref[g]
        y_inter = _dot(cm, s_prev.astype(bf16)) * ea_e[:, g * gw:(g + 1) * gw]
        state_ref[g] = s_prev * ea_e[q - 1:q, g * gw:(g + 1) * gw] + _dot_tn(bm, xde[:, g * gw:(g + 1) * gw])
        y_groups.append(y_intra + y_inter)
    y = jnp.concatenate(y_groups, axis=1) + x * dsk_ref[...]
    y = y * _silu(z_ref[...])
    nw = nw_ref[...]
    out = [_group_rms(y[:, g * gw:(g + 1) * gw], nw[:, g * gw:(g + 1) * gw]) for g in range(SSD_GROUPS)]
    o_ref[...] = jnp.concatenate(out, axis=1).astype(o_ref.dtype)


def _ssd_mixer(proj, bsz, seq, conv_w, conv_b, dt_bias, a_log, d_skip, norm_w):
    q = SSD_Q
    assert seq % q == 0
    nc = seq // q
    head_of_col = jnp.arange(MIX_W) // SSD_HEAD_DIM
    expand = (jnp.arange(LANES)[:, None] == head_of_col[None, :]).astype(bf16)
    row = lambda i, c: (i * nc + c, 0)
    const = lambda i, c: (0, 0)
    return pl.pallas_call(
        _ssd_kernel,
        grid=(bsz, nc),
        in_specs=[pl.BlockSpec((q, SSD_XBC), lambda i, c: (i * nc + c, COL_SSD_XBC // SSD_XBC)),
                  pl.BlockSpec((q, MIX_W), lambda i, c: (i * nc + c, COL_SSD_Z // MIX_W)),
                  pl.BlockSpec((q, LANES), lambda i, c: (i * nc + c, COL_NARROW // LANES)),
                  pl.BlockSpec((SSD_CONV, SSD_XBC), const),
                  pl.BlockSpec((1, SSD_XBC), const),
                  pl.BlockSpec((1, LANES), const),
                  pl.BlockSpec((1, LANES), const),
                  pl.BlockSpec((1, MIX_W), const),
                  pl.BlockSpec((1, MIX_W), const),
                  pl.BlockSpec((LANES, MIX_W), const)],
        out_specs=pl.BlockSpec((q, MIX_W), row),
        out_shape=jax.ShapeDtypeStruct((bsz * seq, MIX_W), bf16),
        scratch_shapes=[pltpu.VMEM((q + SUBLANES, SSD_XBC), f32),
                        pltpu.VMEM((SSD_GROUPS, SSD_STATE, MIX_W // SSD_GROUPS), f32)],
        compiler_params=pltpu.CompilerParams(
            dimension_semantics=("parallel", "arbitrary"), vmem_limit_bytes=VMEM_LIMIT_BYTES),
        name="ssd_mixer",
    )(proj, proj, proj, conv_w.astype(f32), conv_b.reshape(1, SSD_XBC).astype(f32),
      _pad_lanes(dt_bias, LANE_SSD_DT), _pad_lanes(a_log, LANE_SSD_DT),
      jnp.repeat(d_skip.astype(f32), SSD_HEAD_DIM).reshape(1, MIX_W), norm_w.reshape(1, MIX_W).astype(f32), expand)


def _s5_tables(lam_re, lam_im, b_re, b_im, c_re, c_im, log_dt, scan_len):
    hp = lax.Precision.HIGHEST
    t, gb, nb, nh, ns = S5_T, S5_GB, S5_GROUPS // S5_GB, S5_GROUP_CH, S5_STATE
    dt = jnp.exp(log_dt.astype(f32))[:, None]
    ar, ai = lam_re.astype(f32) * dt, lam_im.astype(f32) * dt
    mag = jnp.exp(ar)
    lb_re, lb_im = mag * jnp.cos(ai), mag * jnp.sin(ai)
    den = lam_re * lam_re + lam_im * lam_im
    f_re = ((lb_re - 1.0) * lam_re + lb_im * lam_im) / den
    f_im = (lb_im * lam_re - (lb_re - 1.0) * lam_im) / den
    bb_re = f_re[..., None] * b_re - f_im[..., None] * b_im
    bb_im = f_re[..., None] * b_im + f_im[..., None] * b_re
    d = jnp.arange(t + 1, dtype=f32)[:, None, None]
    pw_mag = jnp.exp(d * ar)
    pw_re, pw_im = pw_mag * jnp.cos(d * ai), pw_mag * jnp.sin(d * ai)
    cr, ci = c_re.astype(f32), c_im.astype(f32)
    m_re = cr[None] * pw_re[:, :, None, :] - ci[None] * pw_im[:, :, None, :]
    m_im = cr[None] * pw_im[:, :, None, :] + ci[None] * pw_re[:, :, None, :]
    kern = (jnp.einsum('dgop,gpi->dgoi', m_re[:t], bb_re, precision=hp)
            - jnp.einsum('dgop,gpi->dgoi', m_im[:t], bb_im, precision=hp))
    eye = jnp.eye(gb, dtype=f32)

    def block_diag(x):
        shape_eye = [1] * (x.ndim + 1)
        shape_eye[2] = gb
        shape_eye[-2] = gb
        return x[..., None, :] * eye.reshape(shape_eye)

    lagk = kern.reshape(t, nb, gb, nh, nh).transpose(1, 0, 2, 4, 3)
    lagk = block_diag(lagk).reshape(nb, t, gb * nh, gb * nh)
    d_rev = jnp.arange(t - 1, -1, -1).astype(f32)[:, None, None]
    rev_mag = jnp.exp(d_rev * ar)
    rev_re, rev_im = rev_mag * jnp.cos(d_rev * ai), rev_mag * jnp.sin(d_rev * ai)
    v_re = rev_re[..., None] * bb_re[None] - rev_im[..., None] * bb_im[None]
    v_im = rev_re[..., None] * bb_im[None] + rev_im[..., None] * bb_re[None]
    vmat = jnp.stack([v_re, v_im], axis=0).reshape(2, t, nb, gb, ns, nh)
    vmat = vmat.transpose(2, 1, 3, 5, 0, 4)
    wmat = jnp.stack([m_re[1:], -m_im[1:]], axis=0).reshape(2, t, nb, gb, nh, ns)
    wmat = wmat.transpose(2, 0, 3, 5, 1, 4)

    def spread_groups(compact, outer, inner, row_unit):
        k = jnp.arange(outer * inner)
        q = jnp.arange(outer * gb * inner)
        copy = ((k[:, None] // inner == q[None, :] // (gb * inner))
                & (k[:, None] % inner == q[None, :] % inner)).astype(bf16)
        wide = jnp.einsum('nrk,kq->nrq', compact, copy, preferred_element_type=f32)
        row_g = (jnp.arange(compact.shape[1]) // row_unit) % gb
        col_g = (q // inner) % gb
        return jnp.where(row_g[:, None] == col_g[None, :], wide, 0.0).astype(bf16)

    vmat = spread_groups(vmat.reshape(nb, t * gb * nh, 2 * ns).astype(bf16), 2, ns, nh)
    wmat = spread_groups(wmat.reshape(nb, 2 * gb * ns, t * nh).astype(bf16), t, nh, ns)
    a_re, a_im = pw_re[t].reshape(nb, gb * ns), pw_im[t].reshape(nb, gb * ns)
    c1, c2 = [], []
    for _ in range(max(1, (scan_len - 1).bit_length())):
        c1.append(jnp.concatenate([a_re, a_re], axis=-1))
        c2.append(jnp.concatenate([-a_im, a_im], axis=-1))
        a_re, a_im = a_re * a_re - a_im * a_im, 2.0 * a_re * a_im
    return lagk.astype(bf16), vmat.astype(bf16), wmat.astype(bf16), jnp.stack(c1, axis=1), jnp.stack(c2, axis=1)


def _s5_scan_kernel(u_ref, lagk_ref, v_ref, w_ref, c1_ref, c2_ref, o_ref, toep_ref):
    t, cps = u_ref.shape[0], u_ref.shape[1]
    half = c1_ref.shape[1] // 2

    @pl.when(pl.program_id(1) == 0)
    def _():
        for j in range(t):
            for tt in range(t):
                blk = lagk_ref[tt - j] if tt >= j else jnp.zeros((LANES, LANES), bf16)
                toep_ref[j * LANES:(j + 1) * LANES, tt * LANES:(tt + 1) * LANES] = blk

    x_all = jnp.concatenate([u_ref[j].astype(bf16) for j in range(t)], axis=1)
    z = _dot(x_all, v_ref[...])
    pos = _iota2((cps, 1), 0)
    c1, c2 = c1_ref[...], c2_ref[...]
    s = z
    for k in range(c1.shape[0]):
        step = 1 << k
        if step >= cps:
            break
        sh = jnp.where(pos >= step, pltpu.roll(s, step, 0), 0.0)
        s = s + sh * c1[k:k + 1, :] + pltpu.roll(sh, half, 1) * c2[k:k + 1, :]
    s_in = jnp.where(pos >= 1, pltpu.roll(s, 1, 0), 0.0)
    y = _dot(x_all, toep_ref[...]) + _dot(s_in.astype(bf16), w_ref[...])
    for j in range(t):
        o_ref[j] = y[:, j * LANES:(j + 1) * LANES]


def _s5_glu_kernel(y_ref, u_ref, d_ref, gw_ref, gb_ref, o_ref):
    y = y_ref[...] + d_ref[...] * u_ref[...]
    y = 0.5 * y * (1.0 + jnp.tanh(math.sqrt(2.0 / math.pi) * (y + 0.044715 * (y * y * y))))
    gate = _dot(y.astype(bf16), gw_ref[...]) + gb_ref[...]
    o_ref[...] = (y * jax.nn.sigmoid(gate)).astype(o_ref.dtype)


def _chunk_major(a, t):
    rows, w = a.shape
    return a.reshape(rows // t, t, w).transpose(1, 0, 2)


def _token_major(a3):
    t, chunks, w = a3.shape
    return a3.transpose(1, 0, 2).reshape(chunks * t, w)


def _s5_mixer(u3, bsz, seq, lam_re, lam_im, b_re, b_im, c_re, c_im, d_skip, log_dt, glu_w, glu_b, tm=512):
    t = S5_T
    assert seq % t == 0 and u3.shape == (t, bsz * seq // t, MIX_W)
    m = bsz * seq
    cps = seq // t
    nb = S5_GROUPS // S5_GB
    lagk, vmat, wmat, c1, c2 = _s5_tables(lam_re, lam_im, b_re, b_im, c_re, c_im, log_dt, cps)
    nk = c1.shape[1]
    tw, sw = t * LANES, 2 * S5_GB * S5_STATE
    per_block = lambda blk, b: (blk, 0, 0)
    seq_block = pl.BlockSpec((t, cps, LANES), lambda blk, b: (0, b, blk))
    y3 = pl.pallas_call(
        _s5_scan_kernel,
        grid=(nb, bsz),
        in_specs=[seq_block,
                  pl.BlockSpec((None, t, LANES, LANES), lambda blk, b: (blk, 0, 0, 0)),
                  pl.BlockSpec((None, tw, sw), per_block),
                  pl.BlockSpec((None, sw, tw), per_block),
                  pl.BlockSpec((None, nk, sw), per_block),
                  pl.BlockSpec((None, nk, sw), per_block)],
        out_specs=seq_block,
        out_shape=jax.ShapeDtypeStruct(u3.shape, f32),
        scratch_shapes=[pltpu.VMEM((tw, tw), bf16)],
        compiler_params=pltpu.CompilerParams(
            dimension_semantics=("arbitrary", "arbitrary"), vmem_limit_bytes=VMEM_LIMIT_BYTES),
        name="s5_scan",
    )(u3, lagk, vmat, wmat, c1, c2)
    assert m % tm == 0
    row_block = pl.BlockSpec((tm, MIX_W), lambda i: (i, 0))
    const = lambda i: (0, 0)
    out = pl.pallas_call(
        _s5_glu_kernel,
        grid=(m // tm,),
        in_specs=[row_block, row_block,
                  pl.BlockSpec((1, MIX_W), const),
                  pl.BlockSpec((MIX_W, MIX_W), const),
                  pl.BlockSpec((1, MIX_W), const)],
        out_specs=row_block,
        out_shape=jax.ShapeDtypeStruct((m, MIX_W), bf16),
        compiler_params=pltpu.CompilerParams(dimension_semantics=("parallel",)),
        name="s5_glu",
    )(y3.reshape(m, MIX_W), u3.reshape(m, MIX_W), d_skip.reshape(1, MIX_W).astype(f32), glu_w.astype(bf16),
      glu_b.reshape(1, MIX_W).astype(f32))
    return _token_major(out.reshape(t, m // t, MIX_W))


def _gla_kernel(qk_ref, v_ref, r_ref, nar_ref, w2_ref, gb_ref, nw_ref, sel_ref, o_ref, state_ref):
    q = v_ref.shape[0]
    levels = q.bit_length() - 1
    dkw = GLA_HEADS * GLA_DK

    @pl.when(pl.program_id(1) == 0)
    def _():
        state_ref[...] = jnp.zeros_like(state_ref)

    gate = _dot(nar_ref[...].astype(bf16), w2_ref[...]) + gb_ref[...]
    log_a = -_softplus(-gate) * (1.0 / GLA_GATE_NORM)
    b = _sel_dot_l(_tri_ones(q), log_a)
    refs = _sel_dot_l(sel_ref[...], b)
    qk = qk_ref[...]
    qs = qk[:, :dkw] * (GLA_DK ** -0.5)
    ks = qk[:, dkw:]
    v = v_ref[...]
    rgate = _silu(r_ref[...])
    nw = nw_ref[...]
    row = _iota2((q, 1), 0)
    ri, ci = _iota2((q, q), 0), _iota2((q, q), 1)
    eb = jnp.exp(b)
    b_end = b[q - 1:q, :]
    k_end = (ks * jnp.exp(b_end - b)).astype(bf16)
    out = []
    for h in range(GLA_HEADS):
        sl = slice(h * GLA_DK, (h + 1) * GLA_DK)
        qh, kh, bh = qs[:, sl], ks[:, sl], b[:, sl]
        vh = v[:, h * GLA_DV:(h + 1) * GLA_DV].astype(bf16)
        scores = jnp.where(ri == ci, jnp.sum(qh * kh, axis=-1, keepdims=True), 0.0)
        for lv in range(levels):
            s = 1 << lv
            upper = ((row >> lv) & 1) == 1
            rl = refs[lv * q:(lv + 1) * q, sl]
            qt = (qh * jnp.exp(jnp.where(upper, bh - rl, -jnp.inf))).astype(bf16)
            kt = (kh * jnp.exp(jnp.where(upper, -jnp.inf, rl - bh))).astype(bf16)
            same_block = (ri >> (lv + 1)) == (ci >> (lv + 1))
            scores = scores + jnp.where(same_block, _dot_nt(qt, kt), 0.0)
        s_prev = state_ref[h]
        o = _dot(scores.astype(bf16), vh) + _dot_nt((qh * eb[:, sl]).astype(bf16), s_prev.astype(bf16))
        state_ref[h] = s_prev * jnp.exp(b_end[:, sl]) + _dot_tn(vh, k_end[:, sl])
        o = _group_rms(o, nw) * rgate[:, h * GLA_DV:(h + 1) * GLA_DV]
        out.append(o)
    o_ref[...] = jnp.concatenate(out, axis=1).astype(o_ref.dtype)


def _gla_mixer(proj, bsz, seq, gate_w2, gate_b, norm_w):
    q = GLA_Q
    assert seq % q == 0 and q & (q - 1) == 0
    nc = seq // q
    levels = q.bit_length() - 1
    i = jnp.arange(q)
    sel = jnp.concatenate(
        [(((i // (2 << lv)) * (2 << lv) + (1 << lv))[:, None] == i[None, :]) for lv in range(levels)], axis=0).astype(bf16)
    w2 = jnp.zeros((LANES, GLA_HEADS * GLA_DK), f32).at[LANE_GLA_G:LANE_GLA_G + GLA_GATE_RANK].set(gate_w2.astype(f32))
    const = lambda i, c: (0, 0)
    blk = lambda col: pl.BlockSpec((q, MIX_W), lambda i, c: (i * nc + c, col // MIX_W))
    return pl.pallas_call(
        _gla_kernel,
        grid=(bsz, nc),
        in_specs=[blk(COL_GLA_QK), blk(COL_GLA_V), blk(COL_GLA_R),
                  pl.BlockSpec((q, LANES), lambda i, c: (i * nc + c, COL_NARROW // LANES)),
                  pl.BlockSpec((LANES, GLA_HEADS * GLA_DK), const),
                  pl.BlockSpec((1, GLA_HEADS * GLA_DK), const),
                  pl.BlockSpec((1, GLA_DV), const),
                  pl.BlockSpec((levels * q, q), const)],
        out_specs=pl.BlockSpec((q, MIX_W), lambda i, c: (i * nc + c, 0)),
        out_shape=jax.ShapeDtypeStruct((bsz * seq, MIX_W), bf16),
        scratch_shapes=[pltpu.VMEM((GLA_HEADS, GLA_DV, GLA_DK), f32)],
        compiler_params=pltpu.CompilerParams(
            dimension_semantics=("parallel", "arbitrary"), vmem_limit_bytes=VMEM_LIMIT_BYTES),
        name="gla_mixer",
    )(proj, proj, proj, proj, w2.astype(bf16), gate_b.reshape(1, -1).astype(f32),
      norm_w.reshape(1, GLA_DV).astype(f32), sel)


def _gdn_kernel(qkv_ref, z_ref, nar_ref, cw_ref, alog_ref, dtb_ref, nw_ref, o_ref, buf_ref, state_ref):
    c = z_ref.shape[0]
    hw = GDN_HEADS * GDN_DK

    @pl.when(pl.program_id(1) == 0)
    def _():
        buf_ref[0:SUBLANES, :] = jnp.zeros((SUBLANES, 3 * MIX_W), f32)
        state_ref[...] = jnp.zeros_like(state_ref)

    qkv = _silu(_causal_conv(qkv_ref[...], buf_ref, cw_ref[...], GDN_CONV))
    nar = nar_ref[...]
    beta = jax.nn.sigmoid(nar)
    g = -jnp.exp(alog_ref[...]) * _softplus(nar + dtb_ref[...])
    gc = _sel_dot_l(_tri_ones(c), g)
    gc_t = gc.T
    eg = jnp.exp(gc)
    eg_rev = jnp.exp(gc[c - 1:c, :] - gc)
    ri, ci = _iota2((c, c), 0), _iota2((c, c), 1)
    z = z_ref[...]
    nw = nw_ref[...]
    heads = range(GDN_HEADS)
    qn, kn, kb16, kh16, decay, egc, x_cur = [], [], [], [], [], [], []
    for h in heads:
        qh = qkv[:, h * GDN_DK:(h + 1) * GDN_DK]
        kh = qkv[:, hw + h * GDN_DK:hw + (h + 1) * GDN_DK]
        vh = qkv[:, 2 * hw + h * GDN_DV:2 * hw + (h + 1) * GDN_DV]
        qn.append(qh * lax.rsqrt(jnp.sum(qh * qh, axis=-1, keepdims=True) + 1e-6) * (GDN_DK ** -0.5))
        kh = kh * lax.rsqrt(jnp.sum(kh * kh, axis=-1, keepdims=True) + 1e-6)
        kn.append(kh)
        bh = beta[:, LANE_GDN_B + h:LANE_GDN_B + h + 1]
        la = LANE_GDN_A + h
        egc.append(eg[:, la:la + 1])
        decay.append(jnp.exp(jnp.where(ri >= ci, gc[:, la:la + 1] - gc_t[la:la + 1, :], -jnp.inf)))
        kb = kh * bh
        kb16.append(kb.astype(bf16))
        kh16.append(kh.astype(bf16))
        x_cur.append(jnp.concatenate([vh * bh, kb * egc[h]], axis=1))
    p_cur = [-jnp.where(ri > ci, _dot_nt(kb16[h], kh16[h]) * decay[h], 0.0) for h in heads]
    n_fac = c.bit_length() - 1
    for k in range(n_fac):
        p_hi = [p.astype(bf16) for p in p_cur]
        px = [_dot(p_hi[h], x_cur[h].astype(bf16)) for h in heads]
        x_cur = [x_cur[h] + px[h] for h in heads]
        if k + 1 < n_fac:
            p_lo = [(p - ph.astype(f32)).astype(bf16) for p, ph in zip(p_cur, p_hi)]
            p_cur = [_dot(p_hi[h], p_hi[h]) + _dot(p_hi[h], p_lo[h]) + _dot(p_lo[h], p_hi[h]) for h in heads]
    s_prev = [state_ref[h] for h in heads]
    s16 = [s.astype(bf16) for s in s_prev]
    v_new = [(x_cur[h][:, :GDN_DV] - _dot(x_cur[h][:, GDN_DV:].astype(bf16), s16[h])).astype(bf16) for h in heads]
    attn = [(_dot_nt(qn[h].astype(bf16), kh16[h]) * decay[h]).astype(bf16) for h in heads]
    o = [_dot((qn[h] * egc[h]).astype(bf16), s16[h]) + _dot(attn[h], v_new[h]) for h in heads]
    for h in heads:
        la = LANE_GDN_A + h
        k_end = (kn[h] * eg_rev[:, la:la + 1]).astype(bf16)
        state_ref[h] = s_prev[h] * eg[c - 1:c, la:la + 1] + _dot_tn(k_end, v_new[h])
    out = [_group_rms(o[h], nw) * _silu(z[:, h * GDN_DV:(h + 1) * GDN_DV]) for h in heads]
    o_ref[...] = jnp.concatenate(out, axis=1).astype(o_ref.dtype)


def _gdn_mixer(proj, bsz, seq, conv_w, a_log, dt_bias, norm_w):
    c = GDN_C
    assert seq % c == 0 and c & (c - 1) == 0
    nc = seq // c
    const = lambda i, cc: (0, 0)
    return pl.pallas_call(
        _gdn_kernel,
        grid=(bsz, nc),
        in_specs=[pl.BlockSpec((c, 3 * MIX_W), lambda i, cc: (i * nc + cc, COL_GDN_QKV // (3 * MIX_W))),
                  pl.BlockSpec((c, MIX_W), lambda i, cc: (i * nc + cc, COL_GDN_Z // MIX_W)),
                  pl.BlockSpec((c, LANES), lambda i, cc: (i * nc + cc, COL_NARROW // LANES)),
                  pl.BlockSpec((GDN_CONV, 3 * MIX_W), const),
                  pl.BlockSpec((1, LANES), const),
                  pl.BlockSpec((1, LANES), const),
                  pl.BlockSpec((1, GDN_DV), const)],
        out_specs=pl.BlockSpec((c, MIX_W), lambda i, cc: (i * nc + cc, 0)),
        out_shape=jax.ShapeDtypeStruct((bsz * seq, MIX_W), bf16),
        scratch_shapes=[pltpu.VMEM((c + SUBLANES, 3 * MIX_W), f32),
                        pltpu.VMEM((GDN_HEADS, GDN_DK, GDN_DV), f32)],
        compiler_params=pltpu.CompilerParams(
            dimension_semantics=("parallel", "arbitrary"), vmem_limit_bytes=VMEM_LIMIT_BYTES),
        name="gdn_mixer",
    )(proj, proj, proj, conv_w.astype(f32), _pad_lanes(a_log, LANE_GDN_A), _pad_lanes(dt_bias, LANE_GDN_A),
      norm_w.reshape(1, GDN_DV).astype(f32))


def kernel(x, norm_mix_pre, norm_mix_post, norm_ffn_pre, norm_ffn_post, w_in, w_out, ssd_conv_w, ssd_conv_b, ssd_dt_bias, ssd_a_log, ssd_d, ssd_norm, s5_lambda_re, s5_lambda_im, s5_b_re, s5_b_im, s5_c_re, s5_c_im, s5_d, s5_log_dt, s5_glu_w, s5_glu_b, gla_gate_w2, gla_gate_b, gla_norm, gdn_conv_w, gdn_a_log, gdn_dt_bias, gdn_norm, ffn_w_gate, ffn_w_up, ffn_conv_w, ffn_w_down):
    bsz, seq, d = x.shape
    m = bsz * seq
    xf = x.reshape(m, d)
    h = _prenorm(xf, norm_mix_pre[0])
    for l in range(DEPTH):
        proj = _matmul(h, _permute_w_in(w_in[l]), f32, 1024, PROJ_TN)
        y_ssd = _ssd_mixer(proj, bsz, seq, ssd_conv_w[l], ssd_conv_b[l], ssd_dt_bias[l], ssd_a_log[l],
                           ssd_d[l], ssd_norm[l])
        h_cm = _chunk_major(h, S5_T).reshape(m, d)
        u3 = _matmul(h_cm, _s5_w_in(w_in[l]), f32, 1024, 1024).reshape(S5_T, m // S5_T, MIX_W)
        y_s5 = _s5_mixer(u3, bsz, seq, s5_lambda_re[l], s5_lambda_im[l], s5_b_re[l], s5_b_im[l],
                         s5_c_re[l], s5_c_im[l], s5_d[l], s5_log_dt[l], s5_glu_w[l], s5_glu_b[l])
        y_gla = _gla_mixer(proj, bsz, seq, gla_gate_w2[l], gla_gate_b[l], gla_norm[l])
        y_gdn = _gdn_mixer(proj, bsz, seq, gdn_conv_w[l], gdn_a_log[l], gdn_dt_bias[l], gdn_norm[l])
        mixed = _outproj([y_ssd, y_s5, y_gla, y_gdn], w_out[l].astype(bf16))
        xf, h = _postnorm_residual_prenorm(xf, mixed, norm_mix_post[l], norm_ffn_pre[l])
        act = _ffn_gate_up(h, ffn_w_gate, ffn_w_up, ffn_conv_w[l], l, seq)
        down = _matmul_ksplit(act, ffn_w_down[l].astype(bf16), f32, 512, 1024, D_FF // 2)
        if l + 1 < DEPTH:
            xf, h = _postnorm_residual_prenorm(xf, down, norm_ffn_post[l], norm_mix_pre[l + 1])
        else:
            xf = _postnorm_residual(xf, down, norm_ffn_post[l])
    return xf.reshape(bsz, seq, d)
```
